```python
import math
import jax, jax.numpy as jnp
from jax import lax
import numpy as np

D_MODEL = 1024
BATCH = 16
SEQ = 2048
DEPTH = 4

CHUNK = 64
N_META = 16
D_MIX = D_MODEL
SB_HEAD_DIM = 64
SB_WIDTH = D_MIX // 2
SB_HEADS = SB_WIDTH // SB_HEAD_DIM
CONF_WIDTH = D_MIX // 4
CONF_KERNEL = 31
SC_WIDTH = D_MIX - SB_WIDTH - CONF_WIDTH
SC_KERNEL = 3
D_IN = 3 * SB_WIDTH + 2 * CONF_WIDTH + 3 * SC_WIDTH
D_FF = 4 * D_MODEL
QBLOCK = 128
DEEPNORM_ALPHA = (2.0 * DEPTH) ** 0.25
DEEPNORM_BETA = (8.0 * DEPTH) ** -0.25
LN_EPS = 1e-5
RMS_EPS = 1e-6

kernel_name = "hymba_stickbreak_conformer_shortconv_deepnorm"


def layer_norm(x, g, b):
    xf = x.astype(jnp.float32)
    mu = jnp.mean(xf, axis=-1, keepdims=True)
    var = jnp.mean(jnp.square(xf - mu), axis=-1, keepdims=True)
    out = (xf - mu) * lax.rsqrt(var + LN_EPS) * g.astype(jnp.float32) + b.astype(jnp.float32)
    return out.astype(x.dtype)


def rms_norm(x, g):
    xf = x.astype(jnp.float32)
    out = xf * lax.rsqrt(jnp.mean(jnp.square(xf), axis=-1, keepdims=True) + RMS_EPS) * g.astype(jnp.float32)
    return out.astype(x.dtype)


def causal_depthwise_conv(x, w):
    k_width, ch = w.shape
    return lax.conv_general_dilated(
        x, w[:, None, :].astype(x.dtype), window_strides=(1,), padding=[(k_width - 1, 0)],
        dimension_numbers=("NWC", "WIO", "NWC"), feature_group_count=ch)


def stick_breaking_attention(q, k, v):
    bsz, n_heads, length, dh = q.shape
    n_blk = -(-length // QBLOCK)
    pad = ((0, 0), (0, 0), (0, n_blk * QBLOCK - length), (0, 0))
    q, k, v = jnp.pad(q, pad), jnp.pad(k, pad), jnp.pad(v, pad)
    scale = dh ** -0.5
    outs = []
    for i in range(n_blk):
        q0, kend = i * QBLOCK, (i + 1) * QBLOCK
        qb = q[:, :, q0:kend].astype(jnp.float32)
        kb = k[:, :, :kend].astype(jnp.float32)
        vb = v[:, :, :kend].astype(jnp.float32)
        z = jnp.einsum("bhtd,bhsd->bhts", qb, kb) * scale
        t_idx = q0 + jnp.arange(QBLOCK)[:, None]
        s_idx = jnp.arange(kend)[None, :]
        past = s_idx < t_idx
        log_keep = jnp.where(past, -jax.nn.softplus(z), 0.0)
        between = lax.cumsum(log_keep, axis=3, reverse=True) - log_keep
        log_a = jax.nn.log_sigmoid(z) + between
        a = jnp.where(past, jnp.exp(log_a), 0.0)
        outs.append(jnp.einsum("bhts,bhsd->bhtd", a, vb))
    o = jnp.concatenate(outs, axis=2)[:, :, :length]
    return o.astype(v.dtype)


def conformer_conv(u, w_dw, b_dw, ln_g, ln_b):
    a, gate = jnp.split(u, 2, axis=-1)
    h = a * jax.nn.sigmoid(gate)
    h = causal_depthwise_conv(h, w_dw) + b_dw.astype(h.dtype)
    h = layer_norm(h, ln_g, ln_b)
    return jax.nn.swish(h)


def short_gated_conv(u, w_dw):
    b_gate, c_gate, h = jnp.split(u, 3, axis=-1)
    return b_gate * causal_depthwise_conv(c_gate * h, w_dw)


def hybrid_mixer(h, w_in, w_conf_dw, b_conf_dw, ln_conf_g, ln_conf_b, w_short_dw, g_mix, w_out):
    bsz, length, _ = h.shape
    u = h @ w_in
    q, k, v, conf_in, sc_in = jnp.split(
        u, [SB_WIDTH, 2 * SB_WIDTH, 3 * SB_WIDTH, 3 * SB_WIDTH + 2 * CONF_WIDTH], axis=-1)

    def to_heads(t):
        return t.reshape(bsz, length, SB_HEADS, SB_HEAD_DIM).transpose(0, 2, 1, 3)

    o_sb = stick_breaking_attention(to_heads(q), to_heads(k), to_heads(v))
    o_sb = o_sb.transpose(0, 2, 1, 3).reshape(bsz, length, SB_WIDTH)
    o_conf = conformer_conv(conf_in, w_conf_dw, b_conf_dw, ln_conf_g, ln_conf_b)
    o_sc = short_gated_conv(sc_in, w_short_dw)
    g_sb, g_conf, g_sc = jnp.split(g_mix, [SB_WIDTH, SB_WIDTH + CONF_WIDTH])
    y = jnp.concatenate([rms_norm(o_sb, g_sb), rms_norm(o_conf, g_conf), rms_norm(o_sc, g_sc)], axis=-1)
    return y @ w_out


def squared_relu_mlp(h, w1, w2):
    return jnp.square(jax.nn.relu(h @ w1)) @ w2


def _fwd_setup_inputs(seed: int = 0) -> dict:
    key = jax.random.key(seed)
    ks = jax.random.split(key, 20)
    f32 = jnp.float32

    def nrm(k, shape, scale):
        return jax.random.normal(k, shape, f32) * scale

    return {
        "x": nrm(ks[0], (BATCH, SEQ, D_MODEL), 1.0),
        "meta_tokens": nrm(ks[1], (N_META, D_MODEL), 1.0),
        "ln_in_g": 1.0 + nrm(ks[2], (D_MODEL,), 0.01),
        "ln_in_b": nrm(ks[3], (D_MODEL,), 0.01),
        "w_in": nrm(ks[4], (DEPTH, D_MODEL, D_IN), D_MODEL ** -0.5),
        "w_conf_dw": nrm(ks[5], (DEPTH, CONF_KERNEL, CONF_WIDTH), CONF_KERNEL ** -0.5),
        "b_conf_dw": nrm(ks[6], (DEPTH, CONF_WIDTH), 0.01),
        "ln_conf_g": 1.0 + nrm(ks[7], (DEPTH, CONF_WIDTH), 0.01),
        "ln_conf_b": nrm(ks[8], (DEPTH, CONF_WIDTH), 0.01),
        "w_short_dw": nrm(ks[9], (DEPTH, SC_KERNEL, SC_WIDTH), SC_KERNEL ** -0.5),
        "g_mix": 1.0 + nrm(ks[10], (DEPTH, D_MIX), 0.01),
        "w_out": nrm(ks[11], (DEPTH, D_MIX, D_MODEL), D_MIX ** -0.5 * DEEPNORM_BETA),
        "ln_mix_g": 1.0 + nrm(ks[12], (DEPTH, D_MODEL), 0.01),
        "ln_mix_b": nrm(ks[13], (DEPTH, D_MODEL), 0.01),
        "w_ff1": nrm(ks[14], (DEPTH, D_MODEL, D_FF), D_MODEL ** -0.5),
        "w_ff2": nrm(ks[15], (DEPTH, D_FF, D_MODEL), D_FF ** -0.5 * DEEPNORM_BETA),
        "ln_ff_g": 1.0 + nrm(ks[16], (DEPTH, D_MODEL), 0.01),
        "ln_ff_b": nrm(ks[17], (DEPTH, D_MODEL), 0.01),
    }


def _fwd_reference(x, meta_tokens, ln_in_g, ln_in_b, w_in, w_conf_dw, b_conf_dw, ln_conf_g, ln_conf_b,
              w_short_dw, g_mix, w_out, ln_mix_g, ln_mix_b, w_ff1, w_ff2, ln_ff_g, ln_ff_b):
    bsz = x.shape[0]
    meta = jnp.broadcast_to(meta_tokens[None].astype(x.dtype), (bsz, N_META, D_MODEL))
    h = layer_norm(jnp.concatenate([meta, x], axis=1), ln_in_g, ln_in_b)
    for l in range(DEPTH):
        mix = hybrid_mixer(h, w_in[l], w_conf_dw[l], b_conf_dw[l], ln_conf_g[l], ln_conf_b[l],
                           w_short_dw[l], g_mix[l], w_out[l])
        h = layer_norm(DEEPNORM_ALPHA * h + mix, ln_mix_g[l], ln_mix_b[l])
        ff = squared_relu_mlp(h, w_ff1[l], w_ff2[l])
        h = layer_norm(DEEPNORM_ALPHA * h + ff, ln_ff_g[l], ln_ff_b[l])
    return h[:, N_META:]


import jax as _jax
import jax.numpy as _jnp

TWIN_FORMAT = 'train_step'
FWD_PARAMS = ['x', 'meta_tokens', 'ln_in_g', 'ln_in_b', 'w_in', 'w_conf_dw', 'b_conf_dw', 'ln_conf_g', 'ln_conf_b', 'w_short_dw', 'g_mix', 'w_out', 'ln_mix_g', 'ln_mix_b', 'w_ff1', 'w_ff2', 'ln_ff_g', 'ln_ff_b']
TWIN_WEIGHTS = ['meta_tokens', 'ln_in_g', 'ln_in_b', 'w_in', 'w_conf_dw', 'b_conf_dw', 'ln_conf_g', 'ln_conf_b', 'w_short_dw', 'g_mix', 'w_out', 'ln_mix_g', 'ln_mix_b', 'w_ff1', 'w_ff2', 'ln_ff_g', 'ln_ff_b']
TWIN_DIFF_INPUT = 'x'
TWIN_INPUTS = ['x', 'meta_tokens', 'ln_in_g', 'ln_in_b', 'w_in', 'w_conf_dw', 'b_conf_dw', 'ln_conf_g', 'ln_conf_b', 'w_short_dw', 'g_mix', 'w_out', 'ln_mix_g', 'ln_mix_b', 'w_ff1', 'w_ff2', 'ln_ff_g', 'ln_ff_b', 'loss_target', 'm_meta_tokens', 'm_ln_in_g', 'm_ln_in_b', 'm_w_in', 'm_w_conf_dw', 'm_b_conf_dw', 'm_ln_conf_g', 'm_ln_conf_b', 'm_w_short_dw', 'm_g_mix', 'm_w_out', 'm_ln_mix_g', 'm_ln_mix_b', 'm_w_ff1', 'm_w_ff2', 'm_ln_ff_g', 'm_ln_ff_b', 'v_meta_tokens', 'v_ln_in_g', 'v_ln_in_b', 'v_w_in', 'v_w_conf_dw', 'v_b_conf_dw', 'v_ln_conf_g', 'v_ln_conf_b', 'v_w_short_dw', 'v_g_mix', 'v_w_out', 'v_ln_mix_g', 'v_ln_mix_b', 'v_w_ff1', 'v_w_ff2', 'v_ln_ff_g', 'v_ln_ff_b']
TWIN_OUTPUTS = ['loss', 'grad_x', 'grad_meta_tokens', 'grad_ln_in_g', 'grad_ln_in_b', 'grad_w_in', 'grad_w_conf_dw', 'grad_b_conf_dw', 'grad_ln_conf_g', 'grad_ln_conf_b', 'grad_w_short_dw', 'grad_g_mix', 'grad_w_out', 'grad_ln_mix_g', 'grad_ln_mix_b', 'grad_w_ff1', 'grad_w_ff2', 'grad_ln_ff_g', 'grad_ln_ff_b', 'delta_meta_tokens', 'delta_ln_in_g', 'delta_ln_in_b', 'delta_w_in', 'delta_w_conf_dw', 'delta_b_conf_dw', 'delta_ln_conf_g', 'delta_ln_conf_b', 'delta_w_short_dw', 'delta_g_mix', 'delta_w_out', 'delta_ln_mix_g', 'delta_ln_mix_b', 'delta_w_ff1', 'delta_w_ff2', 'delta_ln_ff_g', 'delta_ln_ff_b', 'new_m_meta_tokens', 'new_m_ln_in_g', 'new_m_ln_in_b', 'new_m_w_in', 'new_m_w_conf_dw', 'new_m_b_conf_dw', 'new_m_ln_conf_g', 'new_m_ln_conf_b', 'new_m_w_short_dw', 'new_m_g_mix', 'new_m_w_out', 'new_m_ln_mix_g', 'new_m_ln_mix_b', 'new_m_w_ff1', 'new_m_w_ff2', 'new_m_ln_ff_g', 'new_m_ln_ff_b', 'new_v_meta_tokens', 'new_v_ln_in_g', 'new_v_ln_in_b', 'new_v_w_in', 'new_v_w_conf_dw', 'new_v_b_conf_dw', 'new_v_ln_conf_g', 'new_v_ln_conf_b', 'new_v_w_short_dw', 'new_v_g_mix', 'new_v_w_out', 'new_v_ln_mix_g', 'new_v_ln_mix_b', 'new_v_w_ff1', 'new_v_w_ff2', 'new_v_ln_ff_g', 'new_v_ln_ff_b']
TWIN_LEAF_KINDS = {'loss': 'loss', 'grad_x': 'grad_x', 'grad_meta_tokens': 'grad_w', 'grad_ln_in_g': 'grad_w', 'grad_ln_in_b': 'grad_w', 'grad_w_in': 'grad_w', 'grad_w_conf_dw': 'grad_w', 'grad_b_conf_dw': 'grad_w', 'grad_ln_conf_g': 'grad_w', 'grad_ln_conf_b': 'grad_w', 'grad_w_short_dw': 'grad_w', 'grad_g_mix': 'grad_w', 'grad_w_out': 'grad_w', 'grad_ln_mix_g': 'grad_w', 'grad_ln_mix_b': 'grad_w', 'grad_w_ff1': 'grad_w', 'grad_w_ff2': 'grad_w', 'grad_ln_ff_g': 'grad_w', 'grad_ln_ff_b': 'grad_w', 'delta_meta_tokens': 'delta_w', 'delta_ln_in_g': 'delta_w', 'delta_ln_in_b': 'delta_w', 'delta_w_in': 'delta_w', 'delta_w_conf_dw': 'delta_w', 'delta_b_conf_dw': 'delta_w', 'delta_ln_conf_g': 'delta_w', 'delta_ln_conf_b': 'delta_w', 'delta_w_short_dw': 'delta_w', 'delta_g_mix': 'delta_w', 'delta_w_out': 'delta_w', 'delta_ln_mix_g': 'delta_w', 'delta_ln_mix_b': 'delta_w', 'delta_w_ff1': 'delta_w', 'delta_w_ff2': 'delta_w', 'delta_ln_ff_g': 'delta_w', 'delta_ln_ff_b': 'delta_w', 'new_m_meta_tokens': 'new_m', 'new_m_ln_in_g': 'new_m', 'new_m_ln_in_b': 'new_m', 'new_m_w_in': 'new_m', 'new_m_w_conf_dw': 'new_m', 'new_m_b_conf_dw': 'new_m', 'new_m_ln_conf_g': 'new_m', 'new_m_ln_conf_b': 'new_m', 'new_m_w_short_dw': 'new_m', 'new_m_g_mix': 'new_m', 'new_m_w_out': 'new_m', 'new_m_ln_mix_g': 'new_m', 'new_m_ln_mix_b': 'new_m', 'new_m_w_ff1': 'new_m', 'new_m_w_ff2': 'new_m', 'new_m_ln_ff_g': 'new_m', 'new_m_ln_ff_b': 'new_m', 'new_v_meta_tokens': 'new_v', 'new_v_ln_in_g': 'new_v', 'new_v_ln_in_b': 'new_v', 'new_v_w_in': 'new_v', 'new_v_w_conf_dw': 'new_v', 'new_v_b_conf_dw': 'new_v', 'new_v_ln_conf_g': 'new_v', 'new_v_ln_conf_b': 'new_v', 'new_v_w_short_dw': 'new_v', 'new_v_g_mix': 'new_v', 'new_v_w_out': 'new_v', 'new_v_ln_mix_g': 'new_v', 'new_v_ln_mix_b': 'new_v', 'new_v_w_ff1': 'new_v', 'new_v_w_ff2': 'new_v', 'new_v_ln_ff_g': 'new_v', 'new_v_ln_ff_b': 'new_v'}


def _forward(args):
    return _fwd_reference(*[args[k] for k in FWD_PARAMS])


def _output_shape():
    out = _jax.eval_shape(lambda: _forward(_fwd_setup_inputs(0)))
    return out.shape, out.dtype

N_MICROBATCH = 1
ADAM_LR = 0.001
ADAM_B1 = 0.9
ADAM_B2 = 0.999
ADAM_EPS = 1e-08
ADAM_WD = 0.01
ADAM_STEP = 10
PER_EXAMPLE_BATCH_AXIS = {'x': 0, 'loss_target': 0}
SHARED_INPUTS = []
_WEIGHT_DTYPES = {'meta_tokens': _jnp.float32, 'ln_in_g': _jnp.float32, 'ln_in_b': _jnp.float32, 'w_in': _jnp.float32, 'w_conf_dw': _jnp.float32, 'b_conf_dw': _jnp.float32, 'ln_conf_g': _jnp.float32, 'ln_conf_b': _jnp.float32, 'w_short_dw': _jnp.float32, 'g_mix': _jnp.float32, 'w_out': _jnp.float32, 'ln_mix_g': _jnp.float32, 'ln_mix_b': _jnp.float32, 'w_ff1': _jnp.float32, 'w_ff2': _jnp.float32, 'ln_ff_g': _jnp.float32, 'ln_ff_b': _jnp.float32}
MOMENT_SCALE = {'meta_tokens': 3.662434e-03, 'ln_in_g': 2.925593e-01, 'ln_in_b': 3.089458e-01, 'w_in': 3.730650e-02, 'w_conf_dw': 4.850648e-02, 'b_conf_dw': 1.188834e-01, 'ln_conf_g': 6.308434e-02, 'ln_conf_b': 6.554257e-02, 'w_short_dw': 4.628655e-02, 'g_mix': 4.581893e-02, 'w_out': 1.129765e-01, 'ln_mix_g': 3.886949e-01, 'ln_mix_b': 2.800983e-01, 'w_ff1': 3.160842e-02, 'w_ff2': 1.417115e-01, 'ln_ff_g': 1.602160e+01, 'ln_ff_b': 3.807018e+00}


def _to_microbatches(a, axis):
    t = _jnp.moveaxis(a, axis, 0)
    t = t.reshape((N_MICROBATCH, t.shape[0] // N_MICROBATCH) + t.shape[1:])
    return _jnp.moveaxis(t, 1, axis + 1)


def setup_inputs(seed: int = 0) -> dict:
    inp = _fwd_setup_inputs(seed)
    key = _jax.random.fold_in(_jax.random.key(seed), 7919)
    shape, _ = _output_shape()
    out = dict(inp)
    out["loss_target"] = _jax.random.normal(_jax.random.fold_in(key, 0), shape, _jnp.float32)
    for i, name in enumerate(TWIN_WEIGHTS):
        w = inp[name].astype(_jnp.float32)
        if MOMENT_SCALE is None:
            s = _jnp.sqrt(_jnp.mean(_jnp.square(w)) + 1e-30)
        else:
            s = MOMENT_SCALE[name]
        km, kv = _jax.random.split(_jax.random.fold_in(key, i + 1))
        out[name] = w
        out["m_" + name] = s * _jax.random.normal(km, w.shape, _jnp.float32)
        out["v_" + name] = (s * s) * _jax.random.uniform(kv, w.shape, _jnp.float32, 0.5, 1.5)
    if N_MICROBATCH > 1:
        for name, axis in PER_EXAMPLE_BATCH_AXIS.items():
            out[name] = _to_microbatches(out[name], axis)
    return {'x': out['x'], 'meta_tokens': out['meta_tokens'], 'ln_in_g': out['ln_in_g'], 'ln_in_b': out['ln_in_b'], 'w_in': out['w_in'], 'w_conf_dw': out['w_conf_dw'], 'b_conf_dw': out['b_conf_dw'], 'ln_conf_g': out['ln_conf_g'], 'ln_conf_b': out['ln_conf_b'], 'w_short_dw': out['w_short_dw'], 'g_mix': out['g_mix'], 'w_out': out['w_out'], 'ln_mix_g': out['ln_mix_g'], 'ln_mix_b': out['ln_mix_b'], 'w_ff1': out['w_ff1'], 'w_ff2': out['w_ff2'], 'ln_ff_g': out['ln_ff_g'], 'ln_ff_b': out['ln_ff_b'], 'loss_target': out['loss_target'], 'm_meta_tokens': out['m_meta_tokens'], 'm_ln_in_g': out['m_ln_in_g'], 'm_ln_in_b': out['m_ln_in_b'], 'm_w_in': out['m_w_in'], 'm_w_conf_dw': out['m_w_conf_dw'], 'm_b_conf_dw': out['m_b_conf_dw'], 'm_ln_conf_g': out['m_ln_conf_g'], 'm_ln_conf_b': out['m_ln_conf_b'], 'm_w_short_dw': out['m_w_short_dw'], 'm_g_mix': out['m_g_mix'], 'm_w_out': out['m_w_out'], 'm_ln_mix_g': out['m_ln_mix_g'], 'm_ln_mix_b': out['m_ln_mix_b'], 'm_w_ff1': out['m_w_ff1'], 'm_w_ff2': out['m_w_ff2'], 'm_ln_ff_g': out['m_ln_ff_g'], 'm_ln_ff_b': out['m_ln_ff_b'], 'v_meta_tokens': out['v_meta_tokens'], 'v_ln_in_g': out['v_ln_in_g'], 'v_ln_in_b': out['v_ln_in_b'], 'v_w_in': out['v_w_in'], 'v_w_conf_dw': out['v_w_conf_dw'], 'v_b_conf_dw': out['v_b_conf_dw'], 'v_ln_conf_g': out['v_ln_conf_g'], 'v_ln_conf_b': out['v_ln_conf_b'], 'v_w_short_dw': out['v_w_short_dw'], 'v_g_mix': out['v_g_mix'], 'v_w_out': out['v_w_out'], 'v_ln_mix_g': out['v_ln_mix_g'], 'v_ln_mix_b': out['v_ln_mix_b'], 'v_w_ff1': out['v_w_ff1'], 'v_w_ff2': out['v_w_ff2'], 'v_ln_ff_g': out['v_ln_ff_g'], 'v_ln_ff_b': out['v_ln_ff_b']}


def _loss(weights, diff, rest, loss_target):
    with _jax.named_scope("forward"):
        args = {**rest, TWIN_DIFF_INPUT: diff, **{k: w.astype(_WEIGHT_DTYPES[k]) for k, w in weights.items()}}
        y = _forward(args)
    with _jax.named_scope("loss_head"):
        err = _jnp.square(y.astype(_jnp.float32) - loss_target)
        return 0.5 * _jnp.sum(_jnp.mean(err, axis=-1)) if err.ndim else 0.5 * err


def _adamw(w, g, m, v):
    m = ADAM_B1 * m + (1.0 - ADAM_B1) * g
    v = ADAM_B2 * v + (1.0 - ADAM_B2) * _jnp.square(g)
    m_hat = m / (1.0 - ADAM_B1 ** ADAM_STEP)
    v_hat = v / (1.0 - ADAM_B2 ** ADAM_STEP)
    delta = -ADAM_LR * (m_hat / (_jnp.sqrt(v_hat) + ADAM_EPS) + ADAM_WD * w)
    return delta, m, v


def reference(x, meta_tokens, ln_in_g, ln_in_b, w_in, w_conf_dw, b_conf_dw, ln_conf_g, ln_conf_b, w_short_dw, g_mix, w_out, ln_mix_g, ln_mix_b, w_ff1, w_ff2, ln_ff_g, ln_ff_b, loss_target, m_meta_tokens, m_ln_in_g, m_ln_in_b, m_w_in, m_w_conf_dw, m_b_conf_dw, m_ln_conf_g, m_ln_conf_b, m_w_short_dw, m_g_mix, m_w_out, m_ln_mix_g, m_ln_mix_b, m_w_ff1, m_w_ff2, m_ln_ff_g, m_ln_ff_b, v_meta_tokens, v_ln_in_g, v_ln_in_b, v_w_in, v_w_conf_dw, v_b_conf_dw, v_ln_conf_g, v_ln_conf_b, v_w_short_dw, v_g_mix, v_w_out, v_ln_mix_g, v_ln_mix_b, v_w_ff1, v_w_ff2, v_ln_ff_g, v_ln_ff_b):
    given = dict(x=x, meta_tokens=meta_tokens, ln_in_g=ln_in_g, ln_in_b=ln_in_b, w_in=w_in, w_conf_dw=w_conf_dw, b_conf_dw=b_conf_dw, ln_conf_g=ln_conf_g, ln_conf_b=ln_conf_b, w_short_dw=w_short_dw, g_mix=g_mix, w_out=w_out, ln_mix_g=ln_mix_g, ln_mix_b=ln_mix_b, w_ff1=w_ff1, w_ff2=w_ff2, ln_ff_g=ln_ff_g, ln_ff_b=ln_ff_b, loss_target=loss_target, m_meta_tokens=m_meta_tokens, m_ln_in_g=m_ln_in_g, m_ln_in_b=m_ln_in_b, m_w_in=m_w_in, m_w_conf_dw=m_w_conf_dw, m_b_conf_dw=m_b_conf_dw, m_ln_conf_g=m_ln_conf_g, m_ln_conf_b=m_ln_conf_b, m_w_short_dw=m_w_short_dw, m_g_mix=m_g_mix, m_w_out=m_w_out, m_ln_mix_g=m_ln_mix_g, m_ln_mix_b=m_ln_mix_b, m_w_ff1=m_w_ff1, m_w_ff2=m_w_ff2, m_ln_ff_g=m_ln_ff_g, m_ln_ff_b=m_ln_ff_b, v_meta_tokens=v_meta_tokens, v_ln_in_g=v_ln_in_g, v_ln_in_b=v_ln_in_b, v_w_in=v_w_in, v_w_conf_dw=v_w_conf_dw, v_b_conf_dw=v_b_conf_dw, v_ln_conf_g=v_ln_conf_g, v_ln_conf_b=v_ln_conf_b, v_w_short_dw=v_w_short_dw, v_g_mix=v_g_mix, v_w_out=v_w_out, v_ln_mix_g=v_ln_mix_g, v_ln_mix_b=v_ln_mix_b, v_w_ff1=v_w_ff1, v_w_ff2=v_w_ff2, v_ln_ff_g=v_ln_ff_g, v_ln_ff_b=v_ln_ff_b)
    weights = {n: given[n] for n in TWIN_WEIGHTS}
    shared = {n: given[n] for n in SHARED_INPUTS}
    per_example = {n: given[n] for n in ['x']}
    grad_fn = _jax.value_and_grad(_loss, argnums=(0, 1))

    def one_microbatch(ex, loss_target):
        ex = dict(ex)
        diff = ex.pop(TWIN_DIFF_INPUT)
        return grad_fn(weights, diff, {**shared, **ex}, loss_target)

    if N_MICROBATCH == 1:
        loss, (grad_w, grad_x) = one_microbatch(per_example, given["loss_target"])
    else:
        def body(carry, xs):
            loss_sum, grad_sum = carry
            l_k, (gw_k, gx_k) = one_microbatch(xs[0], xs[1])
            with _jax.named_scope("update"):
                return (loss_sum + l_k, _jax.tree.map(_jnp.add, grad_sum, gw_k)), gx_k

        init = (_jnp.zeros((), _jnp.float32), _jax.tree.map(_jnp.zeros_like, weights))
        (loss, grad_w), grad_x = _jax.lax.scan(body, init, (per_example, given["loss_target"]))
    with _jax.named_scope("update"):
        delta_w, new_m, new_v = {}, {}, {}
        for n in TWIN_WEIGHTS:
            delta_w[n], new_m[n], new_v[n] = _adamw(weights[n], grad_w[n], given["m_" + n], given["v_" + n])
    return (loss, grad_x, *[grad_w[n] for n in TWIN_WEIGHTS], *[delta_w[n] for n in TWIN_WEIGHTS],
            *[new_m[n] for n in TWIN_WEIGHTS], *[new_v[n] for n in TWIN_WEIGHTS])
```

```python
import functools

import jax
import jax.numpy as jnp
from jax import lax
from jax.experimental import pallas as pl
from jax.experimental.pallas import tpu as pltpu

F32 = jnp.float32
MM_DTYPE = jnp.float32
SPLIT_DTYPE = jnp.bfloat16
COMM_GRAD_DTYPE = jnp.bfloat16
N_DEV = 8
N_META = 16
HEAD_DIM = 64
LANE = 128
QBLK = 128
HALO = 32
CONV_ROWS = 128
ADAM_BLOCK_ELEMS = 128 * 1024
LN_EPS = 1e-5
RMS_EPS = 1e-6
ADAM_LR = 0.001
ADAM_B1 = 0.9
ADAM_B2 = 0.999
ADAM_EPS = 1e-08
ADAM_WD = 0.01
ADAM_STEP = 10
VMEM_LIMIT_BYTES = 48 * 1024 * 1024
MESH_AXES = ("x", "y", "c")
MESH_IDS = pl.DeviceIdType.MESH


def _cparams(*sem):
    return pltpu.CompilerParams(dimension_semantics=tuple(sem), vmem_limit_bytes=VMEM_LIMIT_BYTES)


def _tile(n, pref, mult=LANE):
    best = None
    t = mult
    while t <= min(n, pref):
        if n % t == 0:
            best = t
        t += mult
    return best if best is not None else n


def _row_tile(t_rows):
    for cand in (272, 256, 128):
        if t_rows % cand == 0:
            return cand
    raise ValueError(t_rows)


def _matmul(a, b, mode, out_dtype, name, layer=None, epilogue=None, extra=None, pm=544, pn=1024, pk=1024):
    if mode == "nn":
        m, k = a.shape
        n = b.shape[-1]
    elif mode == "nt":
        m, k = a.shape
        n = b.shape[-2]
    else:
        k, m = a.shape
        n = b.shape[-1]
    row_mult = 16
    tm = _tile(m, pm, row_mult if mode != "tn" else LANE)
    tn = _tile(n, pn)
    tk = _tile(k, pk, LANE if mode != "tn" else row_mult)
    nk = k // tk
    dn = {"nn": (((1,), (0,)), ((), ())), "nt": (((1,), (1,)), ((), ())), "tn": (((0,), (0,)), ((), ()))}[mode]

    def body(*refs):
        if extra is not None:
            a_ref, b_ref, e_ref, o_ref, acc_ref = refs
        else:
            a_ref, b_ref, o_ref, acc_ref = refs
        kk = pl.program_id(2)

        @pl.when(kk == 0)
        def _():
            acc_ref[...] = jnp.zeros_like(acc_ref)

        acc_ref[...] += _dot3(_split(a_ref[...]), _split(b_ref[...]), dn[0])

        @pl.when(kk == nk - 1)
        def _():
            r = acc_ref[...]
            if epilogue == "relu2":
                r = jnp.square(jnp.maximum(r, 0.0))
            elif epilogue == "dsqrelu":
                r = r * (2.0 * jnp.sqrt(e_ref[...].astype(F32)))
            o_ref[...] = r.astype(o_ref.dtype)

    if mode == "tn":
        a_spec = pl.BlockSpec((tk, tm), lambda i, j, kk: (kk, i))
    else:
        a_spec = pl.BlockSpec((tm, tk), lambda i, j, kk: (i, kk))
    if mode == "nt":
        bshape, bidx = (tn, tk), (lambda i, j, kk: (j, kk))
    else:
        bshape, bidx = (tk, tn), (lambda i, j, kk: (kk, j))
    if layer is not None:
        b_spec = pl.BlockSpec((None,) + bshape, lambda i, j, kk: (layer,) + bidx(i, j, kk))
    else:
        b_spec = pl.BlockSpec(bshape, bidx)
    in_specs = [a_spec, b_spec]
    args = [a, b]
    if extra is not None:
        in_specs.append(pl.BlockSpec((tm, tn), lambda i, j, kk: (i, j)))
        args.append(extra)
    return pl.pallas_call(
        body, name=name,
        out_shape=jax.ShapeDtypeStruct((m, n), out_dtype),
        grid=(m // tm, n // tn, nk),
        in_specs=in_specs,
        out_specs=pl.BlockSpec((tm, tn), lambda i, j, kk: (i, j)),
        scratch_shapes=[pltpu.VMEM((tm, tn), F32)],
        compiler_params=_cparams("parallel", "parallel", "arbitrary"),
    )(*args)


def _ln_fwd(a, b, alpha, gamma, beta, tr, name):
    rows, d = a.shape

    def body(*refs):
        if b is not None:
            a_ref, b_ref, g_ref, be_ref, y_ref, yb_ref = refs
            r = alpha * a_ref[...] + b_ref[...]
        else:
            a_ref, g_ref, be_ref, y_ref, yb_ref = refs
            r = a_ref[...]
        mu = jnp.mean(r, axis=-1, keepdims=True)
        xc = r - mu
        var = jnp.mean(xc * xc, axis=-1, keepdims=True)
        y = xc * lax.rsqrt(var + LN_EPS) * g_ref[...] + be_ref[...]
        y_ref[...] = y
        yb_ref[...] = y.astype(yb_ref.dtype)

    row = pl.BlockSpec((tr, d), lambda i: (i, 0))
    vec = pl.BlockSpec((1, d), lambda i: (0, 0))
    args = [a] + ([b] if b is not None else []) + [gamma.reshape(1, d), beta.reshape(1, d)]
    in_specs = [row] + ([row] if b is not None else []) + [vec, vec]
    return pl.pallas_call(
        body, name=name,
        out_shape=(jax.ShapeDtypeStruct((rows, d), F32), jax.ShapeDtypeStruct((rows, d), MM_DTYPE)),
        grid=(rows // tr,), in_specs=in_specs, out_specs=(row, row),
        compiler_params=_cparams("parallel"),
    )(*args)


def _ln_bwd(a, b, alpha, gamma, dy1, dy2, alpha_dy, tr, name, meta_rows=0, tiles_per_seq=1):
    rows, d = a.shape

    def body(*refs):
        refs = list(refs)
        a_ref = refs.pop(0)
        b_ref = refs.pop(0) if b is not None else None
        g_ref = refs.pop(0)
        dy1_ref = refs.pop(0)
        dy2_ref = refs.pop(0) if dy2 is not None else None
        dr_ref, drb_ref, dg_ref, db_ref = refs[:4]
        i = pl.program_id(0)
        r = a_ref[...] if b is None else alpha * a_ref[...] + b_ref[...]
        dy = dy1_ref[...] if dy2 is None else alpha_dy * dy1_ref[...] + dy2_ref[...]
        mu = jnp.mean(r, axis=-1, keepdims=True)
        xc = r - mu
        var = jnp.mean(xc * xc, axis=-1, keepdims=True)
        rstd = lax.rsqrt(var + LN_EPS)
        xhat = xc * rstd
        dxh = dy * g_ref[...]
        m1 = jnp.mean(dxh, axis=-1, keepdims=True)
        m2 = jnp.mean(dxh * xhat, axis=-1, keepdims=True)
        dr = rstd * (dxh - m1 - xhat * m2)
        dr_ref[...] = dr
        drb_ref[...] = dr.astype(drb_ref.dtype)

        @pl.when(i == 0)
        def _():
            dg_ref[...] = jnp.zeros_like(dg_ref)
            db_ref[...] = jnp.zeros_like(db_ref)
            if meta_rows:
                refs[4][...] = jnp.zeros_like(refs[4])

        dg_ref[...] += jnp.sum(dy * xhat, axis=0, keepdims=True)
        db_ref[...] += jnp.sum(dy, axis=0, keepdims=True)
        if meta_rows:
            @pl.when(i % tiles_per_seq == 0)
            def _():
                refs[4][...] += dr[:meta_rows, :]

    row = pl.BlockSpec((tr, d), lambda i: (i, 0))
    vec = pl.BlockSpec((1, d), lambda i: (0, 0))
    args = [a] + ([b] if b is not None else []) + [gamma.reshape(1, d), dy1] + ([dy2] if dy2 is not None else [])
    in_specs = [row] + ([row] if b is not None else []) + [vec, row] + ([row] if dy2 is not None else [])
    out_shape = [jax.ShapeDtypeStruct((rows, d), F32), jax.ShapeDtypeStruct((rows, d), MM_DTYPE),
                 jax.ShapeDtypeStruct((1, d), F32), jax.ShapeDtypeStruct((1, d), F32)]
    out_specs = [row, row, vec, vec]
    if meta_rows:
        out_shape.append(jax.ShapeDtypeStruct((meta_rows, d), F32))
        out_specs.append(pl.BlockSpec((meta_rows, d), lambda i: (0, 0)))
    return pl.pallas_call(
        body, name=name, out_shape=tuple(out_shape), grid=(rows // tr,), in_specs=in_specs,
        out_specs=tuple(out_specs), compiler_params=_cparams("arbitrary"),
    )(*args)


def _loss(y, tgt, seq, tr, tiles_per_seq, name):
    rows, d = y.shape

    def body(y_ref, t_ref, loss_ref, dy_ref):
        i = pl.program_id(0)
        t_in_seq = (i % tiles_per_seq) * tr + lax.broadcasted_iota(jnp.int32, (tr, 1), 0)
        valid = (t_in_seq >= N_META) & (t_in_seq < N_META + seq)
        diff = jnp.where(valid, y_ref[...] - t_ref[...], 0.0)
        dy_ref[...] = diff * (1.0 / d)

        @pl.when(i == 0)
        def _():
            loss_ref[...] = jnp.zeros_like(loss_ref)

        loss_ref[...] += jnp.full(loss_ref.shape, 0.5 / d, F32) * jnp.sum(diff * diff)

    row = pl.BlockSpec((tr, d), lambda i: (i, 0))
    return pl.pallas_call(
        body, name=name,
        out_shape=(jax.ShapeDtypeStruct((1, LANE), F32), jax.ShapeDtypeStruct((rows, d), F32)),
        grid=(rows // tr,), in_specs=[row, row],
        out_specs=(pl.BlockSpec((1, LANE), lambda i: (0, 0)), row),
        compiler_params=_cparams("arbitrary"),
    )(y, tgt)


def _dot(a, b, dims):
    return lax.dot_general(a, b, (dims, ((), ())), preferred_element_type=F32)


def _split(x):
    hi = x.astype(SPLIT_DTYPE)
    lo = (x - hi.astype(F32)).astype(SPLIT_DTYPE)
    return hi, lo


def _dot3(ap, bp, dims):
    return _dot(ap[0], bp[0], dims) + (_dot(ap[0], bp[1], dims) + _dot(ap[1], bp[0], dims))


def _tri_sum(x, tri):
    hi, lo = _split(x)
    return _dot(hi, tri, ((1,), (0,))) + _dot(lo, tri, ((1,), (0,)))


def _tri_masks():
    row = lax.broadcasted_iota(jnp.int32, (QBLK, QBLK), 0)
    col = lax.broadcasted_iota(jnp.int32, (QBLK, QBLK), 1)
    return row, col


def _attn_fwd(u, n_seq, t_rows, cw, name):
    rows = u.shape[0]
    nb = t_rows // QBLK
    pairs = (2 * cw) // LANE
    kcol, vcol = (2 * cw) // LANE, (4 * cw) // LANE

    def body(q_ref, k_ref, v_ref, o_ref):
        i = pl.program_id(2)
        row, col = _tri_masks()
        incl = (row >= col).astype(SPLIT_DTYPE)
        past = col < row
        outs = []
        for hd in range(LANE // HEAD_DIM):
            sl = slice(hd * HEAD_DIM, (hd + 1) * HEAD_DIM)
            qp = _split(q_ref[:, sl] * (HEAD_DIM ** -0.5))

            def tile(j, carry, acc, masked):
                r0 = pl.multiple_of(j * QBLK, QBLK)
                kp = _split(k_ref[pl.ds(r0, QBLK), sl])
                vp = _split(v_ref[pl.ds(r0, QBLK), sl])
                z = _dot3(qp, kp, ((1,), (1,)))
                e = jnp.exp(-jnp.abs(z))
                lk = -(jnp.maximum(z, 0.0) + jnp.log1p(e))
                if masked:
                    lk = jnp.where(past, lk, 0.0)
                cl = _tri_sum(lk, incl)
                a = jnp.exp(z + cl + carry)
                if masked:
                    a = jnp.where(past, a, 0.0)
                acc = acc + _dot3(_split(a), vp, ((1,), (0,)))
                return carry + cl[:, 0:1], acc

            carry, acc = tile(i, jnp.zeros((QBLK, 1), F32), jnp.zeros((QBLK, HEAD_DIM), F32), True)
            carry, acc = lax.fori_loop(
                0, i, lambda jj, c: tile(i - 1 - jj, c[0], c[1], False), (carry, acc))
            outs.append(acc)
        o_ref[...] = jnp.concatenate(outs, axis=1)

    q_spec = pl.BlockSpec((QBLK, LANE), lambda s, p, i: (s * nb + i, p))
    k_spec = pl.BlockSpec((t_rows, LANE), lambda s, p, i: (s, kcol + p))
    v_spec = pl.BlockSpec((t_rows, LANE), lambda s, p, i: (s, vcol + p))
    return pl.pallas_call(
        body, name=name,
        out_shape=jax.ShapeDtypeStruct((rows, 2 * cw), F32),
        grid=(n_seq, pairs, nb), in_specs=[q_spec, k_spec, v_spec],
        out_specs=pl.BlockSpec((QBLK, LANE), lambda s, p, i: (s * nb + i, p)),
        compiler_params=_cparams("parallel", "parallel", "arbitrary"),
    )(u, u, u)


def _attn_bwd(u, o, do, n_seq, t_rows, cw, name):
    rows = u.shape[0]
    nb = t_rows // QBLK
    pairs = (2 * cw) // LANE
    kcol, vcol = (2 * cw) // LANE, (4 * cw) // LANE
    n_hd = LANE // HEAD_DIM

    def body(q_ref, k_ref, v_ref, o_ref, do_ref, dq_ref, dk_ref, dv_ref, dk_acc, dv_acc):
        i = pl.program_id(2)
        row, col = _tri_masks()
        incl = (row >= col).astype(SPLIT_DTYPE)
        strict = (row > col).astype(SPLIT_DTYPE)
        past = col < row

        @pl.when(i == 0)
        def _():
            dk_acc[...] = jnp.zeros_like(dk_acc)
            dv_acc[...] = jnp.zeros_like(dv_acc)

        dqs = []
        for hd in range(n_hd):
            sl = slice(hd * HEAD_DIM, (hd + 1) * HEAD_DIM)
            qp = _split(q_ref[:, sl] * (HEAD_DIM ** -0.5))
            doh = do_ref[:, sl]
            dop = _split(doh)
            dtot = jnp.sum(doh * o_ref[:, sl], axis=-1, keepdims=True)

            def tile(j, cc, cg, dq, masked):
                r0 = pl.multiple_of(j * QBLK, QBLK)
                kp = _split(k_ref[pl.ds(r0, QBLK), sl])
                vp = _split(v_ref[pl.ds(r0, QBLK), sl])
                z = _dot3(qp, kp, ((1,), (1,)))
                e = jnp.exp(-jnp.abs(z))
                lk = -(jnp.maximum(z, 0.0) + jnp.log1p(e))
                sig = jnp.where(z >= 0.0, 1.0, e) / (1.0 + e)
                if masked:
                    lk = jnp.where(past, lk, 0.0)
                cl = _tri_sum(lk, incl)
                a = jnp.exp(z + cl + cc)
                if masked:
                    a = jnp.where(past, a, 0.0)
                g = a * _dot3(dop, vp, ((1,), (1,)))
                gx = _tri_sum(g, strict)
                dz = g - sig * (dtot - cg - gx)
                if masked:
                    dz = jnp.where(past, dz, 0.0)
                dzp = _split(dz)
                dq = dq + _dot3(dzp, kp, ((1,), (0,)))
                dk_acc[hd, pl.ds(r0, QBLK), :] += _dot3(dzp, qp, ((0,), (0,)))
                dv_acc[hd, pl.ds(r0, QBLK), :] += _dot3(_split(a), dop, ((0,), (0,)))
                return cc + cl[:, 0:1], cg + gx[:, 0:1] + g[:, 0:1], dq

            z1 = jnp.zeros((QBLK, 1), F32)
            cc, cg, dq = tile(i, z1, z1, jnp.zeros((QBLK, HEAD_DIM), F32), True)
            cc, cg, dq = lax.fori_loop(
                0, i, lambda jj, c: tile(i - 1 - jj, c[0], c[1], c[2], False), (cc, cg, dq))
            dqs.append(dq * (HEAD_DIM ** -0.5))
        dq_ref[...] = jnp.concatenate(dqs, axis=1).astype(dq_ref.dtype)

        @pl.when(i == nb - 1)
        def _():
            dk_ref[...] = jnp.concatenate([dk_acc[h] for h in range(n_hd)], axis=1).astype(dk_ref.dtype)
            dv_ref[...] = jnp.concatenate([dv_acc[h] for h in range(n_hd)], axis=1).astype(dv_ref.dtype)

    blk = pl.BlockSpec((QBLK, LANE), lambda s, p, i: (s * nb + i, p))
    k_spec = pl.BlockSpec((t_rows, LANE), lambda s, p, i: (s, kcol + p))
    v_spec = pl.BlockSpec((t_rows, LANE), lambda s, p, i: (s, vcol + p))
    seq_blk = pl.BlockSpec((t_rows, LANE), lambda s, p, i: (s, p))
    shp = jax.ShapeDtypeStruct((rows, 2 * cw), MM_DTYPE)
    return pl.pallas_call(
        body, name=name, out_shape=(shp, shp, shp),
        grid=(n_seq, pairs, nb), in_specs=[blk, k_spec, v_spec, blk, blk],
        out_specs=(blk, seq_blk, seq_blk),
        scratch_shapes=[pltpu.VMEM((n_hd, t_rows, HEAD_DIM), F32), pltpu.VMEM((n_hd, t_rows, HEAD_DIM), F32)],
        compiler_params=_cparams("parallel", "parallel", "arbitrary"),
    )(u, u, u, o, do)


def _shifted(xw, shift):
    n = xw.shape[0]
    s = shift % n
    return xw if s == 0 else pltpu.roll(xw, s, 0)


def _causal_conv(xw, w_ref, kw, shift_sign=1):
    acc = None
    for kk in range(kw):
        term = w_ref[kk:kk + 1, :] * _shifted(xw, shift_sign * (kw - 1 - kk))
        acc = term if acc is None else acc + term
    return acc


def _rms(o, g):
    r = lax.rsqrt(jnp.mean(o * o, axis=-1, keepdims=True) + RMS_EPS)
    return o * r * g, r


def _rms_bwd(o, g, dy):
    r = lax.rsqrt(jnp.mean(o * o, axis=-1, keepdims=True) + RMS_EPS)
    gdy = g * dy
    do = gdy * r - o * (r * r * r) * jnp.mean(gdy * o, axis=-1, keepdims=True)
    return do, dy * o * r


def _mix_specs(rows, cw, halo_prev, halo_next):
    tr = CONV_ROWS
    per = tr // HALO
    last = rows // HALO - 1

    def group(cb):
        specs = []
        if halo_prev:
            specs.append(pl.BlockSpec((HALO, cw), lambda i: (jnp.maximum(i * per - 1, 0), cb)))
        specs.append(pl.BlockSpec((tr, cw), lambda i: (i, cb)))
        if halo_next:
            specs.append(pl.BlockSpec((HALO, cw), lambda i: (jnp.minimum((i + 1) * per, last), cb)))
        return specs
    return group


def _window(prev_ref, cur_ref, first):
    return jnp.concatenate([jnp.where(first, 0.0, prev_ref[...]), cur_ref[...]], axis=0)


def _mix_fwd(u, o_sb, w31, b31, lng, lnb, w3, gmix, t_rows, cw, name):
    rows = u.shape[0]
    d = 4 * cw
    tr = CONV_ROWS
    tiles_per_seq = t_rows // tr
    k31, k3 = w31.shape[0], w3.shape[0]
    group = _mix_specs(rows, cw, True, False)

    def body(ap, ac, gp, gc, bc, cp, cc, hp, hc, osb_ref, w31_ref, b31_ref, lng_ref, lnb_ref, w3_ref, gm_ref, y_ref):
        first = (pl.program_id(0) % tiles_per_seq) == 0
        aw = _window(ap, ac, first)
        gw = _window(gp, gc, first)
        hcw = aw * jax.nn.sigmoid(gw)
        yc = (_causal_conv(hcw, w31_ref, k31) + b31_ref[...])[HALO:]
        mu = jnp.mean(yc, axis=-1, keepdims=True)
        xc = yc - mu
        var = jnp.mean(xc * xc, axis=-1, keepdims=True)
        lc = xc * lax.rsqrt(var + LN_EPS) * lng_ref[...] + lnb_ref[...]
        oc = lc * jax.nn.sigmoid(lc)
        pw = _window(cp, cc, first) * _window(hp, hc, first)
        osc = bc[...] * _causal_conv(pw, w3_ref, k3)[HALO:]
        gm = gm_ref[...]
        y_sb, _ = _rms(osb_ref[...], gm[:, :2 * cw])
        y_c, _ = _rms(oc, gm[:, 2 * cw:3 * cw])
        y_s, _ = _rms(osc, gm[:, 3 * cw:])
        y_ref[...] = jnp.concatenate([y_sb, y_c, y_s], axis=1).astype(y_ref.dtype)

    full = lambda arr: pl.BlockSpec(arr.shape, lambda i: (0, 0))
    small = [w31, b31.reshape(1, cw), lng.reshape(1, cw), lnb.reshape(1, cw), w3, gmix.reshape(1, d)]
    in_specs = (group(6) + group(7) + [pl.BlockSpec((tr, cw), lambda i: (i, 8))] + group(9) + group(10)
                + [pl.BlockSpec((tr, 2 * cw), lambda i: (i, 0))] + [full(s) for s in small])
    return pl.pallas_call(
        body, name=name, out_shape=jax.ShapeDtypeStruct((rows, d), MM_DTYPE),
        grid=(rows // tr,), in_specs=in_specs, out_specs=pl.BlockSpec((tr, d), lambda i: (i, 0)),
        compiler_params=_cparams("parallel"),
    )(u, u, u, u, u, u, u, u, u, o_sb, *small)


def _mix_bwd(u, o_sb, dy, w31, b31, lng, lnb, w3, gmix, t_rows, cw, name):
    rows = u.shape[0]
    d = 4 * cw
    tr = CONV_ROWS
    tiles_per_seq = t_rows // tr
    k31, k3 = w31.shape[0], w3.shape[0]
    group = _mix_specs(rows, cw, True, True)
    cur = slice(HALO, HALO + tr)

    def body(*refs):
        refs = list(refs)
        take = lambda n: [refs.pop(0) for _ in range(n)]
        a3, g3, b3, c3, h3, dyc3, dys3 = (take(3) for _ in range(7))
        osb_ref, dysb_ref, w31_ref, b31_ref, lng_ref, lnb_ref, w3_ref, gm_ref = take(8)
        dosb_ref, du_ref, dgm_ref, dw31_ref, db31_ref, dlng_ref, dlnb_ref, dw3_ref = refs
        i = pl.program_id(0)
        first = (i % tiles_per_seq) == 0
        last = (i % tiles_per_seq) == tiles_per_seq - 1
        wrow = lax.broadcasted_iota(jnp.int32, (tr + 2 * HALO, 1), 0)
        keep = jnp.logical_not((first & (wrow < HALO)) | (last & (wrow >= HALO + tr)))

        def win(r3):
            return jnp.concatenate([r3[0][...], r3[1][...], r3[2][...]], axis=0)

        gm = gm_ref[...]
        aw, gw, bw, cwin, hw = win(a3), win(g3), win(b3), win(c3), win(h3)
        dy_c = jnp.where(keep, win(dyc3), 0.0)
        dy_s = jnp.where(keep, win(dys3), 0.0)
        sg = jax.nn.sigmoid(gw)
        hcw = jnp.where(keep, aw * sg, 0.0)
        yc = _causal_conv(hcw, w31_ref, k31) + b31_ref[...]
        mu = jnp.mean(yc, axis=-1, keepdims=True)
        xc = yc - mu
        rstd = lax.rsqrt(jnp.mean(xc * xc, axis=-1, keepdims=True) + LN_EPS)
        nrm = xc * rstd
        lc = nrm * lng_ref[...] + lnb_ref[...]
        sl = jax.nn.sigmoid(lc)
        oc = lc * sl
        d_oc, dg_c = _rms_bwd(oc, gm[:, 2 * cw:3 * cw], dy_c)
        dlc = d_oc * sl * (1.0 + lc * (1.0 - sl))
        dn = dlc * lng_ref[...]
        dyc = rstd * (dn - jnp.mean(dn, axis=-1, keepdims=True) - nrm * jnp.mean(dn * nrm, axis=-1, keepdims=True))
        dyc = jnp.where(keep, dyc, 0.0)
        dhc = _causal_conv(dyc, w31_ref, k31, -1)
        da = dhc * sg
        dgate = dhc * aw * sg * (1.0 - sg)
        pw = jnp.where(keep, cwin * hw, 0.0)
        cs = _causal_conv(pw, w3_ref, k3)
        osc = bw * cs
        d_osc, dg_s = _rms_bwd(osc, gm[:, 3 * cw:], dy_s)
        dcs = jnp.where(keep, d_osc * bw, 0.0)
        dbg = d_osc * cs
        dp = _causal_conv(dcs, w3_ref, k3, -1)
        dcg = dp * hw
        dhs = dp * cwin
        du_ref[...] = jnp.concatenate([x[cur] for x in (da, dgate, dbg, dcg, dhs)], axis=1).astype(du_ref.dtype)
        d_osb, dg_sb = _rms_bwd(osb_ref[...], gm[:, :2 * cw], dysb_ref[...])
        dosb_ref[...] = d_osb

        @pl.when(i == 0)
        def _():
            for r in (dgm_ref, dw31_ref, db31_ref, dlng_ref, dlnb_ref, dw3_ref):
                r[...] = jnp.zeros_like(r)

        csum = lambda x: jnp.sum(x, axis=0, keepdims=True)
        dgm_ref[...] += jnp.concatenate([csum(dg_sb), csum(dg_c[cur]), csum(dg_s[cur])], axis=1)
        db31_ref[...] += csum(dyc[cur])
        dlng_ref[...] += csum((dlc * nrm)[cur])
        dlnb_ref[...] += csum(dlc[cur])
        for kk in range(k31):
            dw31_ref[kk:kk + 1, :] += csum((dyc * _shifted(hcw, k31 - 1 - kk))[cur])
        for kk in range(k3):
            dw3_ref[kk:kk + 1, :] += csum((dcs * _shifted(pw, k3 - 1 - kk))[cur])

    full = lambda arr: pl.BlockSpec(arr.shape, lambda i: (0, 0))
    small = [w31, b31.reshape(1, cw), lng.reshape(1, cw), lnb.reshape(1, cw), w3, gmix.reshape(1, d)]
    in_specs = (group(6) + group(7) + group(8) + group(9) + group(10) + group(2) + group(3)
                + [pl.BlockSpec((tr, 2 * cw), lambda i: (i, 0))] * 2 + [full(s) for s in small])
    acc = lambda r, c: (jax.ShapeDtypeStruct((r, c), F32), pl.BlockSpec((r, c), lambda i: (0, 0)))
    outs = [(jax.ShapeDtypeStruct((rows, 2 * cw), F32), pl.BlockSpec((tr, 2 * cw), lambda i: (i, 0))),
            (jax.ShapeDtypeStruct((rows, 5 * cw), MM_DTYPE), pl.BlockSpec((tr, 5 * cw), lambda i: (i, 0))),
            acc(1, d), acc(k31, cw), acc(1, cw), acc(1, cw), acc(1, cw), acc(k3, cw)]
    return pl.pallas_call(
        body, name=name, out_shape=tuple(o[0] for o in outs), grid=(rows // tr,), in_specs=in_specs,
        out_specs=tuple(o[1] for o in outs), compiler_params=_cparams("arbitrary"),
    )(*([u] * 15), *([dy] * 6), o_sb, dy, *small)


def _position():
    return lax.axis_index("x"), lax.axis_index("y"), lax.axis_index("c")


def _all_gather(x, name):
    def body(x_ref, out_ref, send_sems, recv_sems, local_sem):
        mx, my, mc = _position()
        me, sibling = (mx, my, mc), (mx, my, 1 - mc)
        chips = [(1 - mx, my), (mx, 1 - my), (1 - mx, 1 - my)]

        def slot(px, py, pc):
            return out_ref.at[4 * px + 2 * py + pc]

        def copy(k, block, to, src=None):
            return pltpu.make_async_remote_copy(
                src_ref=slot(*block) if src is None else src, dst_ref=slot(*block),
                send_sem=send_sems.at[k], recv_sem=recv_sems.at[k], device_id=to, device_id_type=MESH_IDS)

        mine = pltpu.make_async_copy(x_ref, slot(*me), local_sem)
        mine.start()
        first = [copy(0, me, sibling, src=x_ref)]
        first += [copy(1 + j, me, (*chip, mc), src=x_ref) for j, chip in enumerate(chips)]
        for cp in first:
            cp.start()
        passed = [copy(4 + j, (*chip, mc), sibling) for j, chip in enumerate(chips)]
        for j, chip in enumerate(chips):
            copy(1 + j, (*chip, mc), me).wait_recv()
            passed[j].start()
        copy(0, sibling, me).wait_recv()
        for j, chip in enumerate(chips):
            copy(4 + j, (*chip, 1 - mc), me).wait_recv()
        for cp in first + passed:
            cp.wait_send()
        mine.wait()

    return pl.pallas_call(
        body, name=name, out_shape=jax.ShapeDtypeStruct((N_DEV,) + x.shape, x.dtype),
        in_specs=[pl.BlockSpec(memory_space=pl.ANY)], out_specs=pl.BlockSpec(memory_space=pl.ANY),
        scratch_shapes=[pltpu.SemaphoreType.DMA((N_DEV - 1,)), pltpu.SemaphoreType.DMA((N_DEV - 1,)),
                        pltpu.SemaphoreType.DMA],
    )(x)


def _all_to_all(sends, name):
    n_w = len(sends)

    def body(*refs):
        send_refs, recv_refs = refs[:n_w], refs[n_w:2 * n_w]
        send_sems, recv_sems = refs[2 * n_w:]
        mx, my, mc = _position()
        copies = []
        for k in range(1, N_DEV):
            px, py, pc = mx ^ (k >> 2), my ^ ((k >> 1) & 1), mc ^ (k & 1)
            for w in range(n_w):
                copies.append(pltpu.make_async_remote_copy(
                    src_ref=send_refs[w].at[4 * px + 2 * py + pc], dst_ref=recv_refs[w].at[k - 1],
                    send_sem=send_sems.at[w * (N_DEV - 1) + k - 1], recv_sem=recv_sems.at[w * (N_DEV - 1) + k - 1],
                    device_id=(px, py, pc), device_id_type=MESH_IDS))
        for cp in copies:
            cp.start()
        for cp in copies:
            cp.wait_recv()
        for cp in copies:
            cp.wait_send()

    hbm = pl.BlockSpec(memory_space=pl.ANY)
    n_sem = n_w * (N_DEV - 1)
    return pl.pallas_call(
        body, name=name,
        out_shape=tuple(jax.ShapeDtypeStruct((N_DEV - 1,) + s.shape[1:], s.dtype) for s in sends),
        in_specs=[hbm] * n_w, out_specs=tuple([hbm] * n_w),
        scratch_shapes=[pltpu.SemaphoreType.DMA((n_sem,)), pltpu.SemaphoreType.DMA((n_sem,))],
    )(*sends)


def _adam_math(w, g, m, v):
    m = ADAM_B1 * m + (1.0 - ADAM_B1) * g
    v = ADAM_B2 * v + (1.0 - ADAM_B2) * (g * g)
    m_hat = m / (1.0 - ADAM_B1 ** ADAM_STEP)
    v_hat = v / (1.0 - ADAM_B2 ** ADAM_STEP)
    delta = -ADAM_LR * (m_hat / (jnp.sqrt(v_hat) + ADAM_EPS) + ADAM_WD * w)
    return delta, m, v


def _adam(g_own, recv, w, m, v, name):
    rows, cols = w.shape
    tr = _tile(rows, max(8, ADAM_BLOCK_ELEMS // cols), 8)

    def body(*refs):
        if recv is not None:
            g_ref, r_ref, w_ref, m_ref, v_ref, go_ref, d_ref, mo_ref, vo_ref = refs
            g = g_ref[...]
            for k in range(recv.shape[0]):
                g = g + r_ref[k].astype(F32)
        else:
            g_ref, w_ref, m_ref, v_ref, go_ref, d_ref, mo_ref, vo_ref = refs
            g = g_ref[...]
        delta, mn, vn = _adam_math(w_ref[...], g, m_ref[...], v_ref[...])
        go_ref[...] = g
        d_ref[...] = delta
        mo_ref[...] = mn
        vo_ref[...] = vn

    blk = pl.BlockSpec((tr, cols), lambda i: (i, 0))
    in_specs = [blk] + ([pl.BlockSpec((recv.shape[0], tr, cols), lambda i: (0, i, 0))] if recv is not None else []) + [blk] * 3
    args = [g_own] + ([recv] if recv is not None else []) + [w, m, v]
    shp = jax.ShapeDtypeStruct((rows, cols), F32)
    return pl.pallas_call(
        body, name=name, out_shape=(shp,) * 4, grid=(rows // tr,), in_specs=in_specs, out_specs=(blk,) * 4,
        compiler_params=_cparams("parallel"),
    )(*args)


def _sum_devices(parts, name):
    n, rows, cols = parts.shape

    def body(p_ref, o_ref):
        acc = p_ref[0]
        for k in range(1, n):
            acc = acc + p_ref[k]
        o_ref[...] = acc

    return pl.pallas_call(
        body, name=name, out_shape=jax.ShapeDtypeStruct((rows, cols), F32),
        in_specs=[pl.BlockSpec(memory_space=pltpu.VMEM)], out_specs=pl.BlockSpec(memory_space=pltpu.VMEM),
    )(parts)


def _pack(arrs, dtype):
    flat = jnp.concatenate([a.reshape(-1).astype(dtype) for a in arrs])
    pad = (-flat.shape[0]) % (16 * LANE)
    return jnp.pad(flat, (0, pad)).reshape(-1, LANE)


def _unpack(flat2d, shapes):
    flat = flat2d.reshape(-1)
    out, off = [], 0
    for s in shapes:
        n = 1
        for dim in s:
            n *= dim
        out.append(flat[off:off + n].reshape(s))
        off += n
    return out


def _unpack_gathered(g, shapes):
    flat = g.reshape(N_DEV, -1)
    out, off = [], 0
    for s in shapes:
        n = 1
        for dim in s:
            n *= dim
        out.append(flat[:, off:off + n].reshape((N_DEV,) + tuple(s)))
        off += n
    return out


def _cols_from_devices(g):
    nd = g.ndim
    perm = tuple(range(1, nd - 1)) + (0, nd - 1)
    t = jnp.transpose(g, perm)
    return t.reshape(t.shape[:-2] + (t.shape[-2] * t.shape[-1],))


def _cols_to_devices(a, dtype):
    c = a.shape[-1] // N_DEV
    t = a.reshape(a.shape[:-1] + (N_DEV, c)).astype(dtype)
    nd = t.ndim
    return jnp.transpose(t, (nd - 2,) + tuple(range(nd - 2)) + (nd - 1,))


def _rows_to_devices(a, dtype):
    nl, r8, c = a.shape
    return jnp.transpose(a.reshape(nl, N_DEV, r8 // N_DEV, c).astype(dtype), (1, 0, 2, 3))


def kernel(x, meta_tokens, ln_in_g, ln_in_b, w_in, w_conf_dw, b_conf_dw, ln_conf_g, ln_conf_b, w_short_dw, g_mix, w_out, ln_mix_g, ln_mix_b, w_ff1, w_ff2, ln_ff_g, ln_ff_b, loss_target, m_meta_tokens, m_ln_in_g, m_ln_in_b, m_w_in, m_w_conf_dw, m_b_conf_dw, m_ln_conf_g, m_ln_conf_b, m_w_short_dw, m_g_mix, m_w_out, m_ln_mix_g, m_ln_mix_b, m_w_ff1, m_w_ff2, m_ln_ff_g, m_ln_ff_b, v_meta_tokens, v_ln_in_g, v_ln_in_b, v_w_in, v_w_conf_dw, v_b_conf_dw, v_ln_conf_g, v_ln_conf_b, v_w_short_dw, v_g_mix, v_w_out, v_ln_mix_g, v_ln_mix_b, v_w_ff1, v_w_ff2, v_ln_ff_g, v_ln_ff_b):
    n_seq, seq, d = x.shape
    depth = w_in.shape[0]
    cw = d // 4
    alpha = (2.0 * depth) ** 0.25
    t_rows = -(-(N_META + seq) // QBLK) * QBLK
    rows = n_seq * t_rows
    tr = _row_tile(t_rows)
    tiles_per_seq = t_rows // tr
    me = 4 * lax.axis_index("x") + 2 * lax.axis_index("y") + lax.axis_index("c")

    small_shards = [meta_tokens, w_conf_dw, w_short_dw]
    sm = _unpack_gathered(_all_gather(_pack(small_shards, F32), "gather_small_params"), [a.shape for a in small_shards])
    meta_full, w31_full, w3_full = (_cols_from_devices(a) for a in sm)
    big_shards = [w_in, w_out, w_ff1, w_ff2]
    bg = _unpack_gathered(_all_gather(_pack(big_shards, MM_DTYPE), "gather_weights"), [a.shape for a in big_shards])
    win_f = _cols_from_devices(bg[0])
    wout_f = jnp.transpose(bg[1], (1, 0, 2, 3)).reshape(depth, d, d)
    wff1_f = _cols_from_devices(bg[2])
    wff2_f = jnp.transpose(bg[3], (1, 0, 2, 3)).reshape(depth, 4 * d, d)

    pad_rows = t_rows - N_META - seq
    xin = jnp.concatenate([jnp.broadcast_to(meta_full[None], (n_seq, N_META, d)), x,
                           jnp.zeros((n_seq, pad_rows, d), F32)], axis=1).reshape(rows, d)
    tgt = jnp.pad(loss_target, ((0, 0), (N_META, pad_rows), (0, 0))).reshape(rows, d)
    h, hb = _ln_fwd(xin, None, 1.0, ln_in_g, ln_in_b, tr, "ln_in")
    saved = []
    for l in range(depth):
        u = _matmul(hb, win_f, "nn", F32, "proj_in", layer=l, pn=1408)
        o_sb = _attn_fwd(u, n_seq, t_rows, cw, "attn_fwd")
        yb = _mix_fwd(u, o_sb, w31_full[l], b_conf_dw[l], ln_conf_g[l], ln_conf_b[l], w3_full[l], g_mix[l],
                      t_rows, cw, "mix_fwd")
        mix = _matmul(yb, wout_f, "nn", F32, "proj_out", layer=l)
        h1, h1b = _ln_fwd(h, mix, alpha, ln_mix_g[l], ln_mix_b[l], tr, "ln_mix")
        act = _matmul(h1b, wff1_f, "nn", MM_DTYPE, "ff1", layer=l, epilogue="relu2")
        ff = _matmul(act, wff2_f, "nn", F32, "ff2", layer=l)
        h2, h2b = _ln_fwd(h1, ff, alpha, ln_ff_g[l], ln_ff_b[l], tr, "ln_ff")
        saved.append((h, hb, u, o_sb, yb, mix, h1, h1b, act, ff))
        h, hb = h2, h2b
    loss_part, dy1 = _loss(h, tgt, seq, tr, tiles_per_seq, "loss")
    loss = lax.psum(loss_part[0, 0], MESH_AXES)

    dy2, a_dy = None, 1.0
    gw_in, gw_out, gw_ff1, gw_ff2 = [], [], [], []
    g_small = {n: [] for n in ("w31", "b31", "lng", "lnb", "w3", "gmix", "lmg", "lmb", "lfg", "lfb")}
    for l in reversed(range(depth)):
        h, hb, u, o_sb, yb, mix, h1, h1b, act, ff = saved[l]
        dr2, dr2b, dg, db = _ln_bwd(h1, ff, alpha, ln_ff_g[l], dy1, dy2, a_dy, tr, "ln_ff_bwd")
        g_small["lfg"].append(dg)
        g_small["lfb"].append(db)
        df1b = _matmul(dr2b, wff2_f, "nt", MM_DTYPE, "ff2_dx", layer=l, epilogue="dsqrelu", extra=act)
        gw_ff2.append(_matmul(act, dr2b, "tn", F32, "ff2_dw", pm=1024, pk=544))
        dh1 = _matmul(df1b, wff1_f, "nt", F32, "ff1_dx", layer=l)
        gw_ff1.append(_matmul(h1b, df1b, "tn", F32, "ff1_dw", pm=1024, pk=544))
        dr1, dr1b, dg, db = _ln_bwd(h, mix, alpha, ln_mix_g[l], dr2, dh1, alpha, tr, "ln_mix_bwd")
        g_small["lmg"].append(dg)
        g_small["lmb"].append(db)
        dyr = _matmul(dr1b, wout_f, "nt", F32, "proj_out_dx", layer=l)
        gw_out.append(_matmul(yb, dr1b, "tn", F32, "proj_out_dw", pm=1024, pk=544))
        d_osb, du_conv, dgm, dw31, db31, dlng, dlnb, dw3 = _mix_bwd(
            u, o_sb, dyr, w31_full[l], b_conf_dw[l], ln_conf_g[l], ln_conf_b[l], w3_full[l], g_mix[l],
            t_rows, cw, "mix_bwd")
        for n, val in zip(("gmix", "w31", "b31", "lng", "lnb", "w3"), (dgm, dw31, db31, dlng, dlnb, dw3)):
            g_small[n].append(val)
        dq, dk, dv = _attn_bwd(u, o_sb, d_osb, n_seq, t_rows, cw, "attn_bwd")
        du = jnp.concatenate([dq, dk, dv, du_conv], axis=1)
        dh = _matmul(du, win_f, "nt", F32, "proj_in_dx", layer=l, pk=1408)
        gw_in.append(_matmul(hb, du, "tn", F32, "proj_in_dw", pm=1024, pn=1408, pk=544))
        dy1, dy2, a_dy = dr1, dh, alpha
    dxin, _, dg_in, db_in, dmeta = _ln_bwd(xin, None, 1.0, ln_in_g, dy1, dy2, a_dy, tr, "ln_in_bwd",
                                           meta_rows=N_META, tiles_per_seq=tiles_per_seq)
    grad_x = dxin.reshape(n_seq, t_rows, d)[:, N_META:N_META + seq]

    stack = lambda lst: jnp.stack(lst[::-1])
    gw_in, gw_out, gw_ff1, gw_ff2 = stack(gw_in), stack(gw_out), stack(gw_ff1), stack(gw_ff2)
    sends = [_cols_to_devices(gw_in, COMM_GRAD_DTYPE), _rows_to_devices(gw_out, COMM_GRAD_DTYPE),
             _cols_to_devices(gw_ff1, COMM_GRAD_DTYPE), _rows_to_devices(gw_ff2, COMM_GRAD_DTYPE)]
    recvs = _all_to_all(sends, "exchange_weight_grads")
    own = [lax.dynamic_slice_in_dim(gw_in, me * w_in.shape[2], w_in.shape[2], 2),
           lax.dynamic_slice_in_dim(gw_out, me * w_out.shape[1], w_out.shape[1], 1),
           lax.dynamic_slice_in_dim(gw_ff1, me * w_ff1.shape[2], w_ff1.shape[2], 2),
           lax.dynamic_slice_in_dim(gw_ff2, me * w_ff2.shape[1], w_ff2.shape[1], 1)]
    big = {}
    for nm, g_own, rv, w, m, v in zip(("w_in", "w_out", "w_ff1", "w_ff2"), own, recvs,
                                      (w_in, w_out, w_ff1, w_ff2), (m_w_in, m_w_out, m_w_ff1, m_w_ff2),
                                      (v_w_in, v_w_out, v_w_ff1, v_w_ff2)):
        c = w.shape[-1]
        res = _adam(g_own.reshape(-1, c), rv.reshape(N_DEV - 1, -1, c), w.reshape(-1, c), m.reshape(-1, c),
                    v.reshape(-1, c), "adam_" + nm)
        big[nm] = [r.reshape(w.shape) for r in res]

    sstack = lambda n: jnp.stack(g_small[n][::-1])
    partial = {
        "meta_tokens": dmeta, "ln_in_g": dg_in.reshape(d), "ln_in_b": db_in.reshape(d),
        "w_conf_dw": sstack("w31"), "b_conf_dw": sstack("b31").reshape(depth, cw),
        "ln_conf_g": sstack("lng").reshape(depth, cw), "ln_conf_b": sstack("lnb").reshape(depth, cw),
        "w_short_dw": sstack("w3"), "g_mix": sstack("gmix").reshape(depth, d),
        "ln_mix_g": sstack("lmg").reshape(depth, d), "ln_mix_b": sstack("lmb").reshape(depth, d),
        "ln_ff_g": sstack("lfg").reshape(depth, d), "ln_ff_b": sstack("lfb").reshape(depth, d),
    }
    names = list(partial)
    gathered = _all_gather(_pack([partial[n] for n in names], F32), "gather_small_grads")
    summed = _unpack(_sum_devices(gathered, "sum_small_grads"), [partial[n].shape for n in names])
    g_full = dict(zip(names, summed))
    local = {"meta_tokens": (meta_tokens, m_meta_tokens, v_meta_tokens), "ln_in_g": (ln_in_g, m_ln_in_g, v_ln_in_g),
             "ln_in_b": (ln_in_b, m_ln_in_b, v_ln_in_b), "w_conf_dw": (w_conf_dw, m_w_conf_dw, v_w_conf_dw),
             "b_conf_dw": (b_conf_dw, m_b_conf_dw, v_b_conf_dw), "ln_conf_g": (ln_conf_g, m_ln_conf_g, v_ln_conf_g),
             "ln_conf_b": (ln_conf_b, m_ln_conf_b, v_ln_conf_b), "w_short_dw": (w_short_dw, m_w_short_dw, v_w_short_dw),
             "g_mix": (g_mix, m_g_mix, v_g_mix), "ln_mix_g": (ln_mix_g, m_ln_mix_g, v_ln_mix_g),
             "ln_mix_b": (ln_mix_b, m_ln_mix_b, v_ln_mix_b), "ln_ff_g": (ln_ff_g, m_ln_ff_g, v_ln_ff_g),
             "ln_ff_b": (ln_ff_b, m_ln_ff_b, v_ln_ff_b)}
    g_loc = {}
    for n in names:
        wloc = local[n][0]
        g = g_full[n]
        if g.shape != wloc.shape:
            g = lax.dynamic_slice_in_dim(g, me * wloc.shape[-1], wloc.shape[-1], g.ndim - 1)
        g_loc[n] = g
    shapes = [g_loc[n].shape for n in names]
    _, sd, smo, svo = _adam(_pack([g_loc[n] for n in names], F32), None, _pack([local[n][0] for n in names], F32),
                            _pack([local[n][1] for n in names], F32), _pack([local[n][2] for n in names], F32),
                            "adam_small")
    small_d, small_m, small_v = (dict(zip(names, _unpack(a, shapes))) for a in (sd, smo, svo))

    order = ["meta_tokens", "ln_in_g", "ln_in_b", "w_in", "w_conf_dw", "b_conf_dw", "ln_conf_g", "ln_conf_b",
             "w_short_dw", "g_mix", "w_out", "ln_mix_g", "ln_mix_b", "w_ff1", "w_ff2", "ln_ff_g", "ln_ff_b"]
    grads = [big[n][0] if n in big else g_loc[n] for n in order]
    deltas = [big[n][1] if n in big else small_d[n] for n in order]
    new_m = [big[n][2] if n in big else small_m[n] for n in order]
    new_v = [big[n][3] if n in big else small_v[n] for n in order]
    return (loss, grad_x, *grads, *deltas, *new_m, *new_v)
```

```python
import functools

import jax
import jax.numpy as jnp
from jax import lax
from jax.experimental import pallas as pl
from jax.experimental.pallas import tpu as pltpu

F32 = jnp.float32
MM_DTYPE = jnp.float32
SPLIT_DTYPE = jnp.bfloat16
COMM_GRAD_DTYPE = jnp.bfloat16
N_DEV = 8
N_META = 16
HEAD_DIM = 64
LANE = 128
QBLK = 128
ATTN_LANES = 256
HALO = 32
CONV_ROWS = 128
ADAM_BLOCK_ELEMS = 128 * 1024
LN_EPS = 1e-5
RMS_EPS = 1e-6
ADAM_LR = 0.001
ADAM_B1 = 0.9
ADAM_B2 = 0.999
ADAM_EPS = 1e-08
ADAM_WD = 0.01
ADAM_STEP = 10
VMEM_LIMIT_BYTES = 48 * 1024 * 1024
MESH_AXES = ("x", "y", "c")
MESH_IDS = pl.DeviceIdType.MESH


def _cparams(*sem):
    return pltpu.CompilerParams(dimension_semantics=tuple(sem), vmem_limit_bytes=VMEM_LIMIT_BYTES)


def _tile(n, pref, mult=LANE):
    best = None
    t = mult
    while t <= min(n, pref):
        if n % t == 0:
            best = t
        t += mult
    return best if best is not None else n


def _row_tile(t_rows):
    for cand in (272, 256, 128):
        if t_rows % cand == 0:
            return cand
    raise ValueError(t_rows)


def _matmul(a, b, mode, out_dtype, name, layer=None, epilogue=None, extra=None, pm=544, pn=1024, pk=1024):
    if mode == "nn":
        m, k = a.shape
        n = b.shape[-1]
    elif mode == "nt":
        m, k = a.shape
        n = b.shape[-2]
    else:
        k, m = a.shape
        n = b.shape[-1]
    row_mult = 16
    tm = _tile(m, pm, row_mult if mode != "tn" else LANE)
    tn = _tile(n, pn)
    tk = _tile(k, pk, LANE if mode != "tn" else row_mult)
    nk = k // tk
    dn = {"nn": (((1,), (0,)), ((), ())), "nt": (((1,), (1,)), ((), ())), "tn": (((0,), (0,)), ((), ()))}[mode]

    def body(*refs):
        if extra is not None:
            a_ref, b_ref, e_ref, o_ref, acc_ref = refs
        else:
            a_ref, b_ref, o_ref, acc_ref = refs
        kk = pl.program_id(2)

        @pl.when(kk == 0)
        def _():
            acc_ref[...] = jnp.zeros_like(acc_ref)

        acc_ref[...] += _dot3(_split(a_ref[...]), _split(b_ref[...]), dn[0])

        @pl.when(kk == nk - 1)
        def _():
            r = acc_ref[...]
            if epilogue == "relu2":
                r = jnp.square(jnp.maximum(r, 0.0))
            elif epilogue == "dsqrelu":
                r = r * (2.0 * jnp.sqrt(e_ref[...].astype(F32)))
            o_ref[...] = r.astype(o_ref.dtype)

    if mode == "tn":
        a_spec = pl.BlockSpec((tk, tm), lambda i, j, kk: (kk, i))
    else:
        a_spec = pl.BlockSpec((tm, tk), lambda i, j, kk: (i, kk))
    if mode == "nt":
        bshape, bidx = (tn, tk), (lambda i, j, kk: (j, kk))
    else:
        bshape, bidx = (tk, tn), (lambda i, j, kk: (kk, j))
    if layer is not None:
        b_spec = pl.BlockSpec((None,) + bshape, lambda i, j, kk: (layer,) + bidx(i, j, kk))
    else:
        b_spec = pl.BlockSpec(bshape, bidx)
    in_specs = [a_spec, b_spec]
    args = [a, b]
    if extra is not None:
        in_specs.append(pl.BlockSpec((tm, tn), lambda i, j, kk: (i, j)))
        args.append(extra)
    return pl.pallas_call(
        body, name=name,
        out_shape=jax.ShapeDtypeStruct((m, n), out_dtype),
        grid=(m // tm, n // tn, nk),
        in_specs=in_specs,
        out_specs=pl.BlockSpec((tm, tn), lambda i, j, kk: (i, j)),
        scratch_shapes=[pltpu.VMEM((tm, tn), F32)],
        compiler_params=_cparams("parallel", "parallel", "arbitrary"),
    )(*args)


def _ln_fwd(a, b, alpha, gamma, beta, tr, name):
    rows, d = a.shape

    def body(*refs):
        if b is not None:
            a_ref, b_ref, g_ref, be_ref, y_ref, yb_ref = refs
            r = alpha * a_ref[...] + b_ref[...]
        else:
            a_ref, g_ref, be_ref, y_ref, yb_ref = refs
            r = a_ref[...]
        mu = jnp.mean(r, axis=-1, keepdims=True)
        xc = r - mu
        var = jnp.mean(xc * xc, axis=-1, keepdims=True)
        y = xc * lax.rsqrt(var + LN_EPS) * g_ref[...] + be_ref[...]
        y_ref[...] = y
        yb_ref[...] = y.astype(yb_ref.dtype)

    row = pl.BlockSpec((tr, d), lambda i: (i, 0))
    vec = pl.BlockSpec((1, d), lambda i: (0, 0))
    args = [a] + ([b] if b is not None else []) + [gamma.reshape(1, d), beta.reshape(1, d)]
    in_specs = [row] + ([row] if b is not None else []) + [vec, vec]
    return pl.pallas_call(
        body, name=name,
        out_shape=(jax.ShapeDtypeStruct((rows, d), F32), jax.ShapeDtypeStruct((rows, d), MM_DTYPE)),
        grid=(rows // tr,), in_specs=in_specs, out_specs=(row, row),
        compiler_params=_cparams("parallel"),
    )(*args)


def _ln_bwd(a, b, alpha, gamma, dy1, dy2, alpha_dy, tr, name, meta_rows=0, tiles_per_seq=1):
    rows, d = a.shape

    def body(*refs):
        refs = list(refs)
        a_ref = refs.pop(0)
        b_ref = refs.pop(0) if b is not None else None
        g_ref = refs.pop(0)
        dy1_ref = refs.pop(0)
        dy2_ref = refs.pop(0) if dy2 is not None else None
        dr_ref, drb_ref, dg_ref, db_ref = refs[:4]
        i = pl.program_id(0)
        r = a_ref[...] if b is None else alpha * a_ref[...] + b_ref[...]
        dy = dy1_ref[...] if dy2 is None else alpha_dy * dy1_ref[...] + dy2_ref[...]
        mu = jnp.mean(r, axis=-1, keepdims=True)
        xc = r - mu
        var = jnp.mean(xc * xc, axis=-1, keepdims=True)
        rstd = lax.rsqrt(var + LN_EPS)
        xhat = xc * rstd
        dxh = dy * g_ref[...]
        m1 = jnp.mean(dxh, axis=-1, keepdims=True)
        m2 = jnp.mean(dxh * xhat, axis=-1, keepdims=True)
        dr = rstd * (dxh - m1 - xhat * m2)
        dr_ref[...] = dr
        drb_ref[...] = dr.astype(drb_ref.dtype)

        @pl.when(i == 0)
        def _():
            dg_ref[...] = jnp.zeros_like(dg_ref)
            db_ref[...] = jnp.zeros_like(db_ref)
            if meta_rows:
                refs[4][...] = jnp.zeros_like(refs[4])

        dg_ref[...] += jnp.sum(dy * xhat, axis=0, keepdims=True)
        db_ref[...] += jnp.sum(dy, axis=0, keepdims=True)
        if meta_rows:
            @pl.when(i % tiles_per_seq == 0)
            def _():
                refs[4][...] += dr[:meta_rows, :]

    row = pl.BlockSpec((tr, d), lambda i: (i, 0))
    vec = pl.BlockSpec((1, d), lambda i: (0, 0))
    args = [a] + ([b] if b is not None else []) + [gamma.reshape(1, d), dy1] + ([dy2] if dy2 is not None else [])
    in_specs = [row] + ([row] if b is not None else []) + [vec, row] + ([row] if dy2 is not None else [])
    out_shape = [jax.ShapeDtypeStruct((rows, d), F32), jax.ShapeDtypeStruct((rows, d), MM_DTYPE),
                 jax.ShapeDtypeStruct((1, d), F32), jax.ShapeDtypeStruct((1, d), F32)]
    out_specs = [row, row, vec, vec]
    if meta_rows:
        out_shape.append(jax.ShapeDtypeStruct((meta_rows, d), F32))
        out_specs.append(pl.BlockSpec((meta_rows, d), lambda i: (0, 0)))
    return pl.pallas_call(
        body, name=name, out_shape=tuple(out_shape), grid=(rows // tr,), in_specs=in_specs,
        out_specs=tuple(out_specs), compiler_params=_cparams("arbitrary"),
    )(*args)


def _loss(y, tgt, seq, tr, tiles_per_seq, name):
    rows, d = y.shape

    def body(y_ref, t_ref, loss_ref, dy_ref):
        i = pl.program_id(0)
        t_in_seq = (i % tiles_per_seq) * tr + lax.broadcasted_iota(jnp.int32, (tr, 1), 0)
        valid = (t_in_seq >= N_META) & (t_in_seq < N_META + seq)
        diff = jnp.where(valid, y_ref[...] - t_ref[...], 0.0)
        dy_ref[...] = diff * (1.0 / d)

        @pl.when(i == 0)
        def _():
            loss_ref[...] = jnp.zeros_like(loss_ref)

        loss_ref[...] += jnp.full(loss_ref.shape, 0.5 / d, F32) * jnp.sum(diff * diff)

    row = pl.BlockSpec((tr, d), lambda i: (i, 0))
    return pl.pallas_call(
        body, name=name,
        out_shape=(jax.ShapeDtypeStruct((1, LANE), F32), jax.ShapeDtypeStruct((rows, d), F32)),
        grid=(rows // tr,), in_specs=[row, row],
        out_specs=(pl.BlockSpec((1, LANE), lambda i: (0, 0)), row),
        compiler_params=_cparams("arbitrary"),
    )(y, tgt)


def _dot(a, b, dims):
    return lax.dot_general(a, b, (dims, ((), ())), preferred_element_type=F32)


def _split(x):
    hi = x.astype(SPLIT_DTYPE)
    lo = (x - hi.astype(F32)).astype(SPLIT_DTYPE)
    return hi, lo


def _dot3(ap, bp, dims):
    return _dot(ap[0], bp[0], dims) + (_dot(ap[0], bp[1], dims) + _dot(ap[1], bp[0], dims))


def _tri_sum(x, tri):
    hi, lo = _split(x)
    return _dot(hi, tri, ((1,), (0,))) + _dot(lo, tri, ((1,), (0,)))


def _tri_masks():
    row = lax.broadcasted_iota(jnp.int32, (QBLK, QBLK), 0)
    col = lax.broadcasted_iota(jnp.int32, (QBLK, QBLK), 1)
    return row, col


def _attn_fwd(u, n_seq, t_rows, cw, name):
    rows = u.shape[0]
    nb = t_rows // QBLK
    width = min(2 * cw, ATTN_LANES)
    n_hd = width // HEAD_DIM
    groups = (2 * cw) // width
    kcol, vcol = (2 * cw) // width, (4 * cw) // width

    def body(q_ref, k_ref, v_ref, o_ref):
        i = pl.program_id(2)
        row, col = _tri_masks()
        incl = (row >= col).astype(SPLIT_DTYPE)
        past = col < row
        sls = [slice(hd * HEAD_DIM, (hd + 1) * HEAD_DIM) for hd in range(n_hd)]
        qps = [_split(q_ref[:, sl] * (HEAD_DIM ** -0.5)) for sl in sls]

        def tile(j, carries, accs, masked):
            r0 = pl.multiple_of(j * QBLK, QBLK)
            new_c, new_a = [], []
            for hd, sl in enumerate(sls):
                kp = _split(k_ref[pl.ds(r0, QBLK), sl])
                vp = _split(v_ref[pl.ds(r0, QBLK), sl])
                z = _dot3(qps[hd], kp, ((1,), (1,)))
                e = jnp.exp(-jnp.abs(z))
                lk = -(jnp.maximum(z, 0.0) + jnp.log1p(e))
                if masked:
                    lk = jnp.where(past, lk, 0.0)
                cl = _tri_sum(lk, incl)
                a = jnp.exp(z + cl + carries[hd])
                if masked:
                    a = jnp.where(past, a, 0.0)
                new_a.append(accs[hd] + _dot3(_split(a), vp, ((1,), (0,))))
                new_c.append(carries[hd] + cl[:, 0:1])
            return tuple(new_c), tuple(new_a)

        c0 = tuple(jnp.zeros((QBLK, 1), F32) for _ in sls)
        a0 = tuple(jnp.zeros((QBLK, HEAD_DIM), F32) for _ in sls)
        st = tile(i, c0, a0, True)
        st = lax.fori_loop(0, i, lambda jj, c: tile(i - 1 - jj, c[0], c[1], False), st)
        o_ref[...] = jnp.concatenate(st[1], axis=1)

    q_spec = pl.BlockSpec((QBLK, width), lambda s, p, i: (s * nb + i, p))
    k_spec = pl.BlockSpec((t_rows, width), lambda s, p, i: (s, kcol + p))
    v_spec = pl.BlockSpec((t_rows, width), lambda s, p, i: (s, vcol + p))
    return pl.pallas_call(
        body, name=name,
        out_shape=jax.ShapeDtypeStruct((rows, 2 * cw), F32),
        grid=(n_seq, groups, nb), in_specs=[q_spec, k_spec, v_spec],
        out_specs=pl.BlockSpec((QBLK, width), lambda s, p, i: (s * nb + i, p)),
        compiler_params=_cparams("parallel", "parallel", "arbitrary"),
    )(u, u, u)


def _attn_bwd(u, o, do, n_seq, t_rows, cw, name):
    rows = u.shape[0]
    nb = t_rows // QBLK
    width = min(2 * cw, ATTN_LANES)
    n_hd = width // HEAD_DIM
    groups = (2 * cw) // width
    kcol, vcol = (2 * cw) // width, (4 * cw) // width

    def body(q_ref, k_ref, v_ref, o_ref, do_ref, dq_ref, dk_ref, dv_ref, dk_acc, dv_acc):
        i = pl.program_id(2)
        row, col = _tri_masks()
        incl = (row >= col).astype(SPLIT_DTYPE)
        strict = (row > col).astype(SPLIT_DTYPE)
        past = col < row

        @pl.when(i == 0)
        def _():
            dk_acc[...] = jnp.zeros_like(dk_acc)
            dv_acc[...] = jnp.zeros_like(dv_acc)

        sls = [slice(hd * HEAD_DIM, (hd + 1) * HEAD_DIM) for hd in range(n_hd)]
        qps = [_split(q_ref[:, sl] * (HEAD_DIM ** -0.5)) for sl in sls]
        dops = [_split(do_ref[:, sl]) for sl in sls]
        dtots = [jnp.sum(do_ref[:, sl] * o_ref[:, sl], axis=-1, keepdims=True) for sl in sls]

        def tile(j, ccs, cgs, dqs, masked):
            r0 = pl.multiple_of(j * QBLK, QBLK)
            n_cc, n_cg, n_dq = [], [], []
            for hd, sl in enumerate(sls):
                kp = _split(k_ref[pl.ds(r0, QBLK), sl])
                vp = _split(v_ref[pl.ds(r0, QBLK), sl])
                z = _dot3(qps[hd], kp, ((1,), (1,)))
                e = jnp.exp(-jnp.abs(z))
                lk = -(jnp.maximum(z, 0.0) + jnp.log1p(e))
                sig = jnp.where(z >= 0.0, 1.0, e) / (1.0 + e)
                if masked:
                    lk = jnp.where(past, lk, 0.0)
                cl = _tri_sum(lk, incl)
                a = jnp.exp(z + cl + ccs[hd])
                if masked:
                    a = jnp.where(past, a, 0.0)
                g = a * _dot3(dops[hd], vp, ((1,), (1,)))
                gx = _tri_sum(g, strict)
                dz = g - sig * (dtots[hd] - cgs[hd] - gx)
                if masked:
                    dz = jnp.where(past, dz, 0.0)
                dzp = _split(dz)
                n_dq.append(dqs[hd] + _dot3(dzp, kp, ((1,), (0,))))
                dk_acc[hd, pl.ds(r0, QBLK), :] += _dot3(dzp, qps[hd], ((0,), (0,)))
                dv_acc[hd, pl.ds(r0, QBLK), :] += _dot3(_split(a), dops[hd], ((0,), (0,)))
                n_cc.append(ccs[hd] + cl[:, 0:1])
                n_cg.append(cgs[hd] + gx[:, 0:1] + g[:, 0:1])
            return tuple(n_cc), tuple(n_cg), tuple(n_dq)

        z1 = tuple(jnp.zeros((QBLK, 1), F32) for _ in sls)
        dq0 = tuple(jnp.zeros((QBLK, HEAD_DIM), F32) for _ in sls)
        st = tile(i, z1, z1, dq0, True)
        st = lax.fori_loop(0, i, lambda jj, c: tile(i - 1 - jj, c[0], c[1], c[2], False), st)
        dq_ref[...] = (jnp.concatenate(st[2], axis=1) * (HEAD_DIM ** -0.5)).astype(dq_ref.dtype)

        @pl.when(i == nb - 1)
        def _():
            dk_ref[...] = jnp.concatenate([dk_acc[h] for h in range(n_hd)], axis=1).astype(dk_ref.dtype)
            dv_ref[...] = jnp.concatenate([dv_acc[h] for h in range(n_hd)], axis=1).astype(dv_ref.dtype)

    blk = pl.BlockSpec((QBLK, width), lambda s, p, i: (s * nb + i, p))
    k_spec = pl.BlockSpec((t_rows, width), lambda s, p, i: (s, kcol + p))
    v_spec = pl.BlockSpec((t_rows, width), lambda s, p, i: (s, vcol + p))
    seq_blk = pl.BlockSpec((t_rows, width), lambda s, p, i: (s, p))
    shp = jax.ShapeDtypeStruct((rows, 2 * cw), MM_DTYPE)
    return pl.pallas_call(
        body, name=name, out_shape=(shp, shp, shp),
        grid=(n_seq, groups, nb), in_specs=[blk, k_spec, v_spec, blk, blk],
        out_specs=(blk, seq_blk, seq_blk),
        scratch_shapes=[pltpu.VMEM((n_hd, t_rows, HEAD_DIM), F32), pltpu.VMEM((n_hd, t_rows, HEAD_DIM), F32)],
        compiler_params=_cparams("parallel", "parallel", "arbitrary"),
    )(u, u, u, o, do)


def _shifted(xw, shift):
    n = xw.shape[0]
    s = shift % n
    return xw if s == 0 else pltpu.roll(xw, s, 0)


def _causal_conv(xw, w_ref, kw, shift_sign=1):
    acc = None
    for kk in range(kw):
        term = w_ref[kk:kk + 1, :] * _shifted(xw, shift_sign * (kw - 1 - kk))
        acc = term if acc is None else acc + term
    return acc


def _rms(o, g):
    r = lax.rsqrt(jnp.mean(o * o, axis=-1, keepdims=True) + RMS_EPS)
    return o * r * g, r


def _rms_bwd(o, g, dy):
    r = lax.rsqrt(jnp.mean(o * o, axis=-1, keepdims=True) + RMS_EPS)
    gdy = g * dy
    do = gdy * r - o * (r * r * r) * jnp.mean(gdy * o, axis=-1, keepdims=True)
    return do, dy * o * r


def _mix_specs(rows, cw, halo_prev, halo_next):
    tr = CONV_ROWS
    per = tr // HALO
    last = rows // HALO - 1

    def group(cb):
        specs = []
        if halo_prev:
            specs.append(pl.BlockSpec((HALO, cw), lambda i: (jnp.maximum(i * per - 1, 0), cb)))
        specs.append(pl.BlockSpec((tr, cw), lambda i: (i, cb)))
        if halo_next:
            specs.append(pl.BlockSpec((HALO, cw), lambda i: (jnp.minimum((i + 1) * per, last), cb)))
        return specs
    return group


def _window(prev_ref, cur_ref, first):
    return jnp.concatenate([jnp.where(first, 0.0, prev_ref[...]), cur_ref[...]], axis=0)


def _mix_fwd(u, o_sb, w31, b31, lng, lnb, w3, gmix, t_rows, cw, name):
    rows = u.shape[0]
    d = 4 * cw
    tr = CONV_ROWS
    tiles_per_seq = t_rows // tr
    k31, k3 = w31.shape[0], w3.shape[0]
    group = _mix_specs(rows, cw, True, False)

    def body(ap, ac, gp, gc, bc, cp, cc, hp, hc, osb_ref, w31_ref, b31_ref, lng_ref, lnb_ref, w3_ref, gm_ref, y_ref):
        first = (pl.program_id(0) % tiles_per_seq) == 0
        aw = _window(ap, ac, first)
        gw = _window(gp, gc, first)
        hcw = aw * jax.nn.sigmoid(gw)
        yc = (_causal_conv(hcw, w31_ref, k31) + b31_ref[...])[HALO:]
        mu = jnp.mean(yc, axis=-1, keepdims=True)
        xc = yc - mu
        var = jnp.mean(xc * xc, axis=-1, keepdims=True)
        lc = xc * lax.rsqrt(var + LN_EPS) * lng_ref[...] + lnb_ref[...]
        oc = lc * jax.nn.sigmoid(lc)
        pw = _window(cp, cc, first) * _window(hp, hc, first)
        osc = bc[...] * _causal_conv(pw, w3_ref, k3)[HALO:]
        gm = gm_ref[...]
        y_sb, _ = _rms(osb_ref[...], gm[:, :2 * cw])
        y_c, _ = _rms(oc, gm[:, 2 * cw:3 * cw])
        y_s, _ = _rms(osc, gm[:, 3 * cw:])
        y_ref[...] = jnp.concatenate([y_sb, y_c, y_s], axis=1).astype(y_ref.dtype)

    full = lambda arr: pl.BlockSpec(arr.shape, lambda i: (0, 0))
    small = [w31, b31.reshape(1, cw), lng.reshape(1, cw), lnb.reshape(1, cw), w3, gmix.reshape(1, d)]
    in_specs = (group(6) + group(7) + [pl.BlockSpec((tr, cw), lambda i: (i, 8))] + group(9) + group(10)
                + [pl.BlockSpec((tr, 2 * cw), lambda i: (i, 0))] + [full(s) for s in small])
    return pl.pallas_call(
        body, name=name, out_shape=jax.ShapeDtypeStruct((rows, d), MM_DTYPE),
        grid=(rows // tr,), in_specs=in_specs, out_specs=pl.BlockSpec((tr, d), lambda i: (i, 0)),
        compiler_params=_cparams("parallel"),
    )(u, u, u, u, u, u, u, u, u, o_sb, *small)


def _mix_bwd(u, o_sb, dy, w31, b31, lng, lnb, w3, gmix, t_rows, cw, name):
    rows = u.shape[0]
    d = 4 * cw
    tr = CONV_ROWS
    tiles_per_seq = t_rows // tr
    k31, k3 = w31.shape[0], w3.shape[0]
    group = _mix_specs(rows, cw, True, True)
    cur = slice(HALO, HALO + tr)

    def body(*refs):
        refs = list(refs)
        take = lambda n: [refs.pop(0) for _ in range(n)]
        a3, g3, b3, c3, h3, dyc3, dys3 = (take(3) for _ in range(7))
        osb_ref, dysb_ref, w31_ref, b31_ref, lng_ref, lnb_ref, w3_ref, gm_ref = take(8)
        dosb_ref, du_ref, dgm_ref, dw31_ref, db31_ref, dlng_ref, dlnb_ref, dw3_ref = refs
        i = pl.program_id(0)
        first = (i % tiles_per_seq) == 0
        last = (i % tiles_per_seq) == tiles_per_seq - 1
        wrow = lax.broadcasted_iota(jnp.int32, (tr + 2 * HALO, 1), 0)
        keep = jnp.logical_not((first & (wrow < HALO)) | (last & (wrow >= HALO + tr)))

        def win(r3):
            return jnp.concatenate([r3[0][...], r3[1][...], r3[2][...]], axis=0)

        gm = gm_ref[...]
        aw, gw, bw, cwin, hw = win(a3), win(g3), win(b3), win(c3), win(h3)
        dy_c = jnp.where(keep, win(dyc3), 0.0)
        dy_s = jnp.where(keep, win(dys3), 0.0)
        sg = jax.nn.sigmoid(gw)
        hcw = jnp.where(keep, aw * sg, 0.0)
        yc = _causal_conv(hcw, w31_ref, k31) + b31_ref[...]
        mu = jnp.mean(yc, axis=-1, keepdims=True)
        xc = yc - mu
        rstd = lax.rsqrt(jnp.mean(xc * xc, axis=-1, keepdims=True) + LN_EPS)
        nrm = xc * rstd
        lc = nrm * lng_ref[...] + lnb_ref[...]
        sl = jax.nn.sigmoid(lc)
        oc = lc * sl
        d_oc, dg_c = _rms_bwd(oc, gm[:, 2 * cw:3 * cw], dy_c)
        dlc = d_oc * sl * (1.0 + lc * (1.0 - sl))
        dn = dlc * lng_ref[...]
        dyc = rstd * (dn - jnp.mean(dn, axis=-1, keepdims=True) - nrm * jnp.mean(dn * nrm, axis=-1, keepdims=True))
        dyc = jnp.where(keep, dyc, 0.0)
        dhc = _causal_conv(dyc, w31_ref, k31, -1)
        da = dhc * sg
        dgate = dhc * aw * sg * (1.0 - sg)
        pw = jnp.where(keep, cwin * hw, 0.0)
        cs = _causal_conv(pw, w3_ref, k3)
        osc = bw * cs
        d_osc, dg_s = _rms_bwd(osc, gm[:, 3 * cw:], dy_s)
        dcs = jnp.where(keep, d_osc * bw, 0.0)
        dbg = d_osc * cs
        dp = _causal_conv(dcs, w3_ref, k3, -1)
        dcg = dp * hw
        dhs = dp * cwin
        du_ref[...] = jnp.concatenate([x[cur] for x in (da, dgate, dbg, dcg, dhs)], axis=1).astype(du_ref.dtype)
        d_osb, dg_sb = _rms_bwd(osb_ref[...], gm[:, :2 * cw], dysb_ref[...])
        dosb_ref[...] = d_osb

        @pl.when(i == 0)
        def _():
            for r in (dgm_ref, dw31_ref, db31_ref, dlng_ref, dlnb_ref, dw3_ref):
                r[...] = jnp.zeros_like(r)

        csum = lambda x: jnp.sum(x, axis=0, keepdims=True)
        dgm_ref[...] += jnp.concatenate([csum(dg_sb), csum(dg_c[cur]), csum(dg_s[cur])], axis=1)
        db31_ref[...] += csum(dyc[cur])
        dlng_ref[...] += csum((dlc * nrm)[cur])
        dlnb_ref[...] += csum(dlc[cur])
        for kk in range(k31):
            dw31_ref[kk:kk + 1, :] += csum((dyc * _shifted(hcw, k31 - 1 - kk))[cur])
        for kk in range(k3):
            dw3_ref[kk:kk + 1, :] += csum((dcs * _shifted(pw, k3 - 1 - kk))[cur])

    full = lambda arr: pl.BlockSpec(arr.shape, lambda i: (0, 0))
    small = [w31, b31.reshape(1, cw), lng.reshape(1, cw), lnb.reshape(1, cw), w3, gmix.reshape(1, d)]
    in_specs = (group(6) + group(7) + group(8) + group(9) + group(10) + group(2) + group(3)
                + [pl.BlockSpec((tr, 2 * cw), lambda i: (i, 0))] * 2 + [full(s) for s in small])
    acc = lambda r, c: (jax.ShapeDtypeStruct((r, c), F32), pl.BlockSpec((r, c), lambda i: (0, 0)))
    outs = [(jax.ShapeDtypeStruct((rows, 2 * cw), F32), pl.BlockSpec((tr, 2 * cw), lambda i: (i, 0))),
            (jax.ShapeDtypeStruct((rows, 5 * cw), MM_DTYPE), pl.BlockSpec((tr, 5 * cw), lambda i: (i, 0))),
            acc(1, d), acc(k31, cw), acc(1, cw), acc(1, cw), acc(1, cw), acc(k3, cw)]
    return pl.pallas_call(
        body, name=name, out_shape=tuple(o[0] for o in outs), grid=(rows // tr,), in_specs=in_specs,
        out_specs=tuple(o[1] for o in outs), compiler_params=_cparams("arbitrary"),
    )(*([u] * 15), *([dy] * 6), o_sb, dy, *small)


def _position():
    return lax.axis_index("x"), lax.axis_index("y"), lax.axis_index("c")


def _all_gather(x, name):
    def body(x_ref, out_ref, send_sems, recv_sems, local_sem):
        mx, my, mc = _position()
        me, sibling = (mx, my, mc), (mx, my, 1 - mc)
        chips = [(1 - mx, my), (mx, 1 - my), (1 - mx, 1 - my)]

        def slot(px, py, pc):
            return out_ref.at[4 * px + 2 * py + pc]

        def copy(k, block, to, src=None):
            return pltpu.make_async_remote_copy(
                src_ref=slot(*block) if src is None else src, dst_ref=slot(*block),
                send_sem=send_sems.at[k], recv_sem=recv_sems.at[k], device_id=to, device_id_type=MESH_IDS)

        mine = pltpu.make_async_copy(x_ref, slot(*me), local_sem)
        mine.start()
        first = [copy(0, me, sibling, src=x_ref)]
        first += [copy(1 + j, me, (*chip, mc), src=x_ref) for j, chip in enumerate(chips)]
        for cp in first:
            cp.start()
        passed = [copy(4 + j, (*chip, mc), sibling) for j, chip in enumerate(chips)]
        for j, chip in enumerate(chips):
            copy(1 + j, (*chip, mc), me).wait_recv()
            passed[j].start()
        copy(0, sibling, me).wait_recv()
        for j, chip in enumerate(chips):
            copy(4 + j, (*chip, 1 - mc), me).wait_recv()
        for cp in first + passed:
            cp.wait_send()
        mine.wait()

    return pl.pallas_call(
        body, name=name, out_shape=jax.ShapeDtypeStruct((N_DEV,) + x.shape, x.dtype),
        in_specs=[pl.BlockSpec(memory_space=pl.ANY)], out_specs=pl.BlockSpec(memory_space=pl.ANY),
        scratch_shapes=[pltpu.SemaphoreType.DMA((N_DEV - 1,)), pltpu.SemaphoreType.DMA((N_DEV - 1,)),
                        pltpu.SemaphoreType.DMA],
    )(x)


def _all_to_all(sends, name):
    n_w = len(sends)

    def body(*refs):
        send_refs, recv_refs = refs[:n_w], refs[n_w:2 * n_w]
        send_sems, recv_sems = refs[2 * n_w:]
        mx, my, mc = _position()
        copies = []
        for k in range(1, N_DEV):
            px, py, pc = mx ^ (k >> 2), my ^ ((k >> 1) & 1), mc ^ (k & 1)
            for w in range(n_w):
                copies.append(pltpu.make_async_remote_copy(
                    src_ref=send_refs[w].at[4 * px + 2 * py + pc], dst_ref=recv_refs[w].at[k - 1],
                    send_sem=send_sems.at[w * (N_DEV - 1) + k - 1], recv_sem=recv_sems.at[w * (N_DEV - 1) + k - 1],
                    device_id=(px, py, pc), device_id_type=MESH_IDS))
        for cp in copies:
            cp.start()
        for cp in copies:
            cp.wait_recv()
        for cp in copies:
            cp.wait_send()

    hbm = pl.BlockSpec(memory_space=pl.ANY)
    n_sem = n_w * (N_DEV - 1)
    return pl.pallas_call(
        body, name=name,
        out_shape=tuple(jax.ShapeDtypeStruct((N_DEV - 1,) + s.shape[1:], s.dtype) for s in sends),
        in_specs=[hbm] * n_w, out_specs=tuple([hbm] * n_w),
        scratch_shapes=[pltpu.SemaphoreType.DMA((n_sem,)), pltpu.SemaphoreType.DMA((n_sem,))],
    )(*sends)


def _adam_math(w, g, m, v):
    m = ADAM_B1 * m + (1.0 - ADAM_B1) * g
    v = ADAM_B2 * v + (1.0 - ADAM_B2) * (g * g)
    m_hat = m / (1.0 - ADAM_B1 ** ADAM_STEP)
    v_hat = v / (1.0 - ADAM_B2 ** ADAM_STEP)
    delta = -ADAM_LR * (m_hat / (jnp.sqrt(v_hat) + ADAM_EPS) + ADAM_WD * w)
    return delta, m, v


def _adam(g_own, recv, w, m, v, name):
    rows, cols = w.shape
    tr = _tile(rows, max(8, ADAM_BLOCK_ELEMS // cols), 8)

    def body(*refs):
        if recv is not None:
            g_ref, r_ref, w_ref, m_ref, v_ref, go_ref, d_ref, mo_ref, vo_ref = refs
            g = g_ref[...]
            for k in range(recv.shape[0]):
                g = g + r_ref[k].astype(F32)
        else:
            g_ref, w_ref, m_ref, v_ref, go_ref, d_ref, mo_ref, vo_ref = refs
            g = g_ref[...]
        delta, mn, vn = _adam_math(w_ref[...], g, m_ref[...], v_ref[...])
        go_ref[...] = g
        d_ref[...] = delta
        mo_ref[...] = mn
        vo_ref[...] = vn

    blk = pl.BlockSpec((tr, cols), lambda i: (i, 0))
    in_specs = [blk] + ([pl.BlockSpec((recv.shape[0], tr, cols), lambda i: (0, i, 0))] if recv is not None else []) + [blk] * 3
    args = [g_own] + ([recv] if recv is not None else []) + [w, m, v]
    shp = jax.ShapeDtypeStruct((rows, cols), F32)
    return pl.pallas_call(
        body, name=name, out_shape=(shp,) * 4, grid=(rows // tr,), in_specs=in_specs, out_specs=(blk,) * 4,
        compiler_params=_cparams("parallel"),
    )(*args)


def _sum_devices(parts, name):
    n, rows, cols = parts.shape

    def body(p_ref, o_ref):
        acc = p_ref[0]
        for k in range(1, n):
            acc = acc + p_ref[k]
        o_ref[...] = acc

    return pl.pallas_call(
        body, name=name, out_shape=jax.ShapeDtypeStruct((rows, cols), F32),
        in_specs=[pl.BlockSpec(memory_space=pltpu.VMEM)], out_specs=pl.BlockSpec(memory_space=pltpu.VMEM),
    )(parts)


def _pack(arrs, dtype):
    flat = jnp.concatenate([a.reshape(-1).astype(dtype) for a in arrs])
    pad = (-flat.shape[0]) % (16 * LANE)
    return jnp.pad(flat, (0, pad)).reshape(-1, LANE)


def _unpack(flat2d, shapes):
    flat = flat2d.reshape(-1)
    out, off = [], 0
    for s in shapes:
        n = 1
        for dim in s:
            n *= dim
        out.append(flat[off:off + n].reshape(s))
        off += n
    return out


def _unpack_gathered(g, shapes):
    flat = g.reshape(N_DEV, -1)
    out, off = [], 0
    for s in shapes:
        n = 1
        for dim in s:
            n *= dim
        out.append(flat[:, off:off + n].reshape((N_DEV,) + tuple(s)))
        off += n
    return out


def _cols_from_devices(g):
    nd = g.ndim
    perm = tuple(range(1, nd - 1)) + (0, nd - 1)
    t = jnp.transpose(g, perm)
    return t.reshape(t.shape[:-2] + (t.shape[-2] * t.shape[-1],))


def _cols_to_devices(a, dtype):
    c = a.shape[-1] // N_DEV
    t = a.reshape(a.shape[:-1] + (N_DEV, c)).astype(dtype)
    nd = t.ndim
    return jnp.transpose(t, (nd - 2,) + tuple(range(nd - 2)) + (nd - 1,))


def _rows_to_devices(a, dtype):
    nl, r8, c = a.shape
    return jnp.transpose(a.reshape(nl, N_DEV, r8 // N_DEV, c).astype(dtype), (1, 0, 2, 3))


def kernel(x, meta_tokens, ln_in_g, ln_in_b, w_in, w_conf_dw, b_conf_dw, ln_conf_g, ln_conf_b, w_short_dw, g_mix, w_out, ln_mix_g, ln_mix_b, w_ff1, w_ff2, ln_ff_g, ln_ff_b, loss_target, m_meta_tokens, m_ln_in_g, m_ln_in_b, m_w_in, m_w_conf_dw, m_b_conf_dw, m_ln_conf_g, m_ln_conf_b, m_w_short_dw, m_g_mix, m_w_out, m_ln_mix_g, m_ln_mix_b, m_w_ff1, m_w_ff2, m_ln_ff_g, m_ln_ff_b, v_meta_tokens, v_ln_in_g, v_ln_in_b, v_w_in, v_w_conf_dw, v_b_conf_dw, v_ln_conf_g, v_ln_conf_b, v_w_short_dw, v_g_mix, v_w_out, v_ln_mix_g, v_ln_mix_b, v_w_ff1, v_w_ff2, v_ln_ff_g, v_ln_ff_b):
    n_seq, seq, d = x.shape
    depth = w_in.shape[0]
    cw = d // 4
    alpha = (2.0 * depth) ** 0.25
    t_rows = -(-(N_META + seq) // QBLK) * QBLK
    rows = n_seq * t_rows
    tr = _row_tile(t_rows)
    tiles_per_seq = t_rows // tr
    me = 4 * lax.axis_index("x") + 2 * lax.axis_index("y") + lax.axis_index("c")

    small_shards = [meta_tokens, w_conf_dw, w_short_dw]
    sm = _unpack_gathered(_all_gather(_pack(small_shards, F32), "gather_small_params"), [a.shape for a in small_shards])
    meta_full, w31_full, w3_full = (_cols_from_devices(a) for a in sm)
    big_shards = [w_in, w_out, w_ff1, w_ff2]
    bg = _unpack_gathered(_all_gather(_pack(big_shards, MM_DTYPE), "gather_weights"), [a.shape for a in big_shards])
    win_f = _cols_from_devices(bg[0])
    wout_f = jnp.transpose(bg[1], (1, 0, 2, 3)).reshape(depth, d, d)
    wff1_f = _cols_from_devices(bg[2])
    wff2_f = jnp.transpose(bg[3], (1, 0, 2, 3)).reshape(depth, 4 * d, d)

    pad_rows = t_rows - N_META - seq
    xin = jnp.concatenate([jnp.broadcast_to(meta_full[None], (n_seq, N_META, d)), x,
                           jnp.zeros((n_seq, pad_rows, d), F32)], axis=1).reshape(rows, d)
    tgt = jnp.pad(loss_target, ((0, 0), (N_META, pad_rows), (0, 0))).reshape(rows, d)
    h, hb = _ln_fwd(xin, None, 1.0, ln_in_g, ln_in_b, tr, "ln_in")
    saved = []
    for l in range(depth):
        u = _matmul(hb, win_f, "nn", F32, "proj_in", layer=l, pn=1408)
        o_sb = _attn_fwd(u, n_seq, t_rows, cw, "attn_fwd")
        yb = _mix_fwd(u, o_sb, w31_full[l], b_conf_dw[l], ln_conf_g[l], ln_conf_b[l], w3_full[l], g_mix[l],
                      t_rows, cw, "mix_fwd")
        mix = _matmul(yb, wout_f, "nn", F32, "proj_out", layer=l)
        h1, h1b = _ln_fwd(h, mix, alpha, ln_mix_g[l], ln_mix_b[l], tr, "ln_mix")
        act = _matmul(h1b, wff1_f, "nn", MM_DTYPE, "ff1", layer=l, epilogue="relu2")
        ff = _matmul(act, wff2_f, "nn", F32, "ff2", layer=l)
        h2, h2b = _ln_fwd(h1, ff, alpha, ln_ff_g[l], ln_ff_b[l], tr, "ln_ff")
        saved.append((h, hb, u, o_sb, yb, mix, h1, h1b, act, ff))
        h, hb = h2, h2b
    loss_part, dy1 = _loss(h, tgt, seq, tr, tiles_per_seq, "loss")
    loss = lax.psum(loss_part[0, 0], MESH_AXES)

    dy2, a_dy = None, 1.0
    gw_in, gw_out, gw_ff1, gw_ff2 = [], [], [], []
    g_small = {n: [] for n in ("w31", "b31", "lng", "lnb", "w3", "gmix", "lmg", "lmb", "lfg", "lfb")}
    for l in reversed(range(depth)):
        h, hb, u, o_sb, yb, mix, h1, h1b, act, ff = saved[l]
        dr2, dr2b, dg, db = _ln_bwd(h1, ff, alpha, ln_ff_g[l], dy1, dy2, a_dy, tr, "ln_ff_bwd")
        g_small["lfg"].append(dg)
        g_small["lfb"].append(db)
        df1b = _matmul(dr2b, wff2_f, "nt", MM_DTYPE, "ff2_dx", layer=l, epilogue="dsqrelu", extra=act)
        gw_ff2.append(_matmul(act, dr2b, "tn", F32, "ff2_dw", pm=1024, pk=544))
        dh1 = _matmul(df1b, wff1_f, "nt", F32, "ff1_dx", layer=l)
        gw_ff1.append(_matmul(h1b, df1b, "tn", F32, "ff1_dw", pm=1024, pk=544))
        dr1, dr1b, dg, db = _ln_bwd(h, mix, alpha, ln_mix_g[l], dr2, dh1, alpha, tr, "ln_mix_bwd")
        g_small["lmg"].append(dg)
        g_small["lmb"].append(db)
        dyr = _matmul(dr1b, wout_f, "nt", F32, "proj_out_dx", layer=l)
        gw_out.append(_matmul(yb, dr1b, "tn", F32, "proj_out_dw", pm=1024, pk=544))
        d_osb, du_conv, dgm, dw31, db31, dlng, dlnb, dw3 = _mix_bwd(
            u, o_sb, dyr, w31_full[l], b_conf_dw[l], ln_conf_g[l], ln_conf_b[l], w3_full[l], g_mix[l],
            t_rows, cw, "mix_bwd")
        for n, val in zip(("gmix", "w31", "b31", "lng", "lnb", "w3"), (dgm, dw31, db31, dlng, dlnb, dw3)):
            g_small[n].append(val)
        dq, dk, dv = _attn_bwd(u, o_sb, d_osb, n_seq, t_rows, cw, "attn_bwd")
        du = jnp.concatenate([dq, dk, dv, du_conv], axis=1)
        dh = _matmul(du, win_f, "nt", F32, "proj_in_dx", layer=l, pk=1408)
        gw_in.append(_matmul(hb, du, "tn", F32, "proj_in_dw", pm=1024, pn=1408, pk=544))
        dy1, dy2, a_dy = dr1, dh, alpha
    dxin, _, dg_in, db_in, dmeta = _ln_bwd(xin, None, 1.0, ln_in_g, dy1, dy2, a_dy, tr, "ln_in_bwd",
                                           meta_rows=N_META, tiles_per_seq=tiles_per_seq)
    grad_x = dxin.reshape(n_seq, t_rows, d)[:, N_META:N_META + seq]

    stack = lambda lst: jnp.stack(lst[::-1])
    gw_in, gw_out, gw_ff1, gw_ff2 = stack(gw_in), stack(gw_out), stack(gw_ff1), stack(gw_ff2)
    sends = [_cols_to_devices(gw_in, COMM_GRAD_DTYPE), _rows_to_devices(gw_out, COMM_GRAD_DTYPE),
             _cols_to_devices(gw_ff1, COMM_GRAD_DTYPE), _rows_to_devices(gw_ff2, COMM_GRAD_DTYPE)]
    recvs = _all_to_all(sends, "exchange_weight_grads")
    own = [lax.dynamic_slice_in_dim(gw_in, me * w_in.shape[2], w_in.shape[2], 2),
           lax.dynamic_slice_in_dim(gw_out, me * w_out.shape[1], w_out.shape[1], 1),
           lax.dynamic_slice_in_dim(gw_ff1, me * w_ff1.shape[2], w_ff1.shape[2], 2),
           lax.dynamic_slice_in_dim(gw_ff2, me * w_ff2.shape[1], w_ff2.shape[1], 1)]
    big = {}
    for nm, g_own, rv, w, m, v in zip(("w_in", "w_out", "w_ff1", "w_ff2"), own, recvs,
                                      (w_in, w_out, w_ff1, w_ff2), (m_w_in, m_w_out, m_w_ff1, m_w_ff2),
                                      (v_w_in, v_w_out, v_w_ff1, v_w_ff2)):
        c = w.shape[-1]
        res = _adam(g_own.reshape(-1, c), rv.reshape(N_DEV - 1, -1, c), w.reshape(-1, c), m.reshape(-1, c),
                    v.reshape(-1, c), "adam_" + nm)
        big[nm] = [r.reshape(w.shape) for r in res]

    sstack = lambda n: jnp.stack(g_small[n][::-1])
    partial = {
        "meta_tokens": dmeta, "ln_in_g": dg_in.reshape(d), "ln_in_b": db_in.reshape(d),
        "w_conf_dw": sstack("w31"), "b_conf_dw": sstack("b31").reshape(depth, cw),
        "ln_conf_g": sstack("lng").reshape(depth, cw), "ln_conf_b": sstack("lnb").reshape(depth, cw),
        "w_short_dw": sstack("w3"), "g_mix": sstack("gmix").reshape(depth, d),
        "ln_mix_g": sstack("lmg").reshape(depth, d), "ln_mix_b": sstack("lmb").reshape(depth, d),
        "ln_ff_g": sstack("lfg").reshape(depth, d), "ln_ff_b": sstack("lfb").reshape(depth, d),
    }
    names = list(partial)
    gathered = _all_gather(_pack([partial[n] for n in names], F32), "gather_small_grads")
    summed = _unpack(_sum_devices(gathered, "sum_small_grads"), [partial[n].shape for n in names])
    g_full = dict(zip(names, summed))
    local = {"meta_tokens": (meta_tokens, m_meta_tokens, v_meta_tokens), "ln_in_g": (ln_in_g, m_ln_in_g, v_ln_in_g),
             "ln_in_b": (ln_in_b, m_ln_in_b, v_ln_in_b), "w_conf_dw": (w_conf_dw, m_w_conf_dw, v_w_conf_dw),
             "b_conf_dw": (b_conf_dw, m_b_conf_dw, v_b_conf_dw), "ln_conf_g": (ln_conf_g, m_ln_conf_g, v_ln_conf_g),
             "ln_conf_b": (ln_conf_b, m_ln_conf_b, v_ln_conf_b), "w_short_dw": (w_short_dw, m_w_short_dw, v_w_short_dw),
             "g_mix": (g_mix, m_g_mix, v_g_mix), "ln_mix_g": (ln_mix_g, m_ln_mix_g, v_ln_mix_g),
             "ln_mix_b": (ln_mix_b, m_ln_mix_b, v_ln_mix_b), "ln_ff_g": (ln_ff_g, m_ln_ff_g, v_ln_ff_g),
             "ln_ff_b": (ln_ff_b, m_ln_ff_b, v_ln_ff_b)}
    g_loc = {}
    for n in names:
        wloc = local[n][0]
        g = g_full[n]
        if g.shape != wloc.shape:
            g = lax.dynamic_slice_in_dim(g, me * wloc.shape[-1], wloc.shape[-1], g.ndim - 1)
        g_loc[n] = g
    shapes = [g_loc[n].shape for n in names]
    _, sd, smo, svo = _adam(_pack([g_loc[n] for n in names], F32), None, _pack([local[n][0] for n in names], F32),
                            _pack([local[n][1] for n in names], F32), _pack([local[n][2] for n in names], F32),
                            "adam_small")
    small_d, small_m, small_v = (dict(zip(names, _unpack(a, shapes))) for a in (sd, smo, svo))

    order = ["meta_tokens", "ln_in_g", "ln_in_b", "w_in", "w_conf_dw", "b_conf_dw", "ln_conf_g", "ln_conf_b",
             "w_short_dw", "g_mix", "w_out", "ln_mix_g", "ln_mix_b", "w_ff1", "w_ff2", "ln_ff_g", "ln_ff_b"]
    grads = [big[n][0] if n in big else g_loc[n] for n in order]
    deltas = [big[n][1] if n in big else small_d[n] for n in order]
    new_m = [big[n][2] if n in big else small_m[n] for n in order]
    new_v = [big[n][3] if n in big else small_v[n] for n in order]
    return (loss, grad_x, *grads, *deltas, *new_m, *new_v)
```

```python
import functools

import jax
import jax.numpy as jnp
from jax import lax
from jax.experimental import pallas as pl
from jax.experimental.pallas import tpu as pltpu

F32 = jnp.float32
MM_DTYPE = jnp.float32
SPLIT_DTYPE = jnp.bfloat16
COMM_GRAD_DTYPE = jnp.bfloat16
N_DEV = 8
N_META = 16
HEAD_DIM = 64
LANE = 128
QBLK = 128
ATTN_LANES = 256
HALO = 32
CONV_ROWS = 128
ADAM_BLOCK_ELEMS = 128 * 1024
LN_EPS = 1e-5
RMS_EPS = 1e-6
ADAM_LR = 0.001
ADAM_B1 = 0.9
ADAM_B2 = 0.999
ADAM_EPS = 1e-08
ADAM_WD = 0.01
ADAM_STEP = 10
VMEM_LIMIT_BYTES = 48 * 1024 * 1024
MESH_AXES = ("x", "y", "c")
MESH_IDS = pl.DeviceIdType.MESH


def _cparams(*sem):
    return pltpu.CompilerParams(dimension_semantics=tuple(sem), vmem_limit_bytes=VMEM_LIMIT_BYTES)


def _tile(n, pref, mult=LANE):
    best = None
    t = mult
    while t <= min(n, pref):
        if n % t == 0:
            best = t
        t += mult
    return best if best is not None else n


def _row_tile(t_rows):
    for cand in (272, 256, 128):
        if t_rows % cand == 0:
            return cand
    raise ValueError(t_rows)


def _matmul(a, b, mode, out_dtype, name, layer=None, epilogue=None, extra=None, pm=544, pn=1024, pk=1024):
    if mode == "nn":
        m, k = a.shape
        n = b.shape[-1]
    elif mode == "nt":
        m, k = a.shape
        n = b.shape[-2]
    else:
        k, m = a.shape
        n = b.shape[-1]
    row_mult = 16
    tm = _tile(m, pm, row_mult if mode != "tn" else LANE)
    tn = _tile(n, pn)
    tk = _tile(k, pk, LANE if mode != "tn" else row_mult)
    nk = k // tk
    dn = {"nn": (((1,), (0,)), ((), ())), "nt": (((1,), (1,)), ((), ())), "tn": (((0,), (0,)), ((), ()))}[mode]

    def body(*refs):
        if extra is not None:
            a_ref, b_ref, e_ref, o_ref, acc_ref = refs
        else:
            a_ref, b_ref, o_ref, acc_ref = refs
        kk = pl.program_id(2)

        @pl.when(kk == 0)
        def _():
            acc_ref[...] = jnp.zeros_like(acc_ref)

        acc_ref[...] += _dot3(_split(a_ref[...]), _split(b_ref[...]), dn[0])

        @pl.when(kk == nk - 1)
        def _():
            r = acc_ref[...]
            if epilogue == "relu2":
                r = jnp.square(jnp.maximum(r, 0.0))
            elif epilogue == "dsqrelu":
                r = r * (2.0 * jnp.sqrt(e_ref[...].astype(F32)))
            o_ref[...] = r.astype(o_ref.dtype)

    if mode == "tn":
        a_spec = pl.BlockSpec((tk, tm), lambda i, j, kk: (kk, i))
    else:
        a_spec = pl.BlockSpec((tm, tk), lambda i, j, kk: (i, kk))
    if mode == "nt":
        bshape, bidx = (tn, tk), (lambda i, j, kk: (j, kk))
    else:
        bshape, bidx = (tk, tn), (lambda i, j, kk: (kk, j))
    if layer is not None:
        b_spec = pl.BlockSpec((None,) + bshape, lambda i, j, kk: (layer,) + bidx(i, j, kk))
    else:
        b_spec = pl.BlockSpec(bshape, bidx)
    in_specs = [a_spec, b_spec]
    args = [a, b]
    if extra is not None:
        in_specs.append(pl.BlockSpec((tm, tn), lambda i, j, kk: (i, j)))
        args.append(extra)
    return pl.pallas_call(
        body, name=name,
        out_shape=jax.ShapeDtypeStruct((m, n), out_dtype),
        grid=(m // tm, n // tn, nk),
        in_specs=in_specs,
        out_specs=pl.BlockSpec((tm, tn), lambda i, j, kk: (i, j)),
        scratch_shapes=[pltpu.VMEM((tm, tn), F32)],
        compiler_params=_cparams("parallel", "parallel", "arbitrary"),
    )(*args)


def _ln_fwd(a, b, alpha, gamma, beta, tr, name):
    rows, d = a.shape

    def body(*refs):
        if b is not None:
            a_ref, b_ref, g_ref, be_ref, y_ref, yb_ref = refs
            r = alpha * a_ref[...] + b_ref[...]
        else:
            a_ref, g_ref, be_ref, y_ref, yb_ref = refs
            r = a_ref[...]
        mu = jnp.mean(r, axis=-1, keepdims=True)
        xc = r - mu
        var = jnp.mean(xc * xc, axis=-1, keepdims=True)
        y = xc * lax.rsqrt(var + LN_EPS) * g_ref[...] + be_ref[...]
        y_ref[...] = y
        yb_ref[...] = y.astype(yb_ref.dtype)

    row = pl.BlockSpec((tr, d), lambda i: (i, 0))
    vec = pl.BlockSpec((1, d), lambda i: (0, 0))
    args = [a] + ([b] if b is not None else []) + [gamma.reshape(1, d), beta.reshape(1, d)]
    in_specs = [row] + ([row] if b is not None else []) + [vec, vec]
    return pl.pallas_call(
        body, name=name,
        out_shape=(jax.ShapeDtypeStruct((rows, d), F32), jax.ShapeDtypeStruct((rows, d), MM_DTYPE)),
        grid=(rows // tr,), in_specs=in_specs, out_specs=(row, row),
        compiler_params=_cparams("parallel"),
    )(*args)


def _ln_bwd(a, b, alpha, gamma, dy1, dy2, alpha_dy, tr, name, meta_rows=0, tiles_per_seq=1):
    rows, d = a.shape

    def body(*refs):
        refs = list(refs)
        a_ref = refs.pop(0)
        b_ref = refs.pop(0) if b is not None else None
        g_ref = refs.pop(0)
        dy1_ref = refs.pop(0)
        dy2_ref = refs.pop(0) if dy2 is not None else None
        dr_ref, drb_ref, dg_ref, db_ref = refs[:4]
        i = pl.program_id(0)
        r = a_ref[...] if b is None else alpha * a_ref[...] + b_ref[...]
        dy = dy1_ref[...] if dy2 is None else alpha_dy * dy1_ref[...] + dy2_ref[...]
        mu = jnp.mean(r, axis=-1, keepdims=True)
        xc = r - mu
        var = jnp.mean(xc * xc, axis=-1, keepdims=True)
        rstd = lax.rsqrt(var + LN_EPS)
        xhat = xc * rstd
        dxh = dy * g_ref[...]
        m1 = jnp.mean(dxh, axis=-1, keepdims=True)
        m2 = jnp.mean(dxh * xhat, axis=-1, keepdims=True)
        dr = rstd * (dxh - m1 - xhat * m2)
        dr_ref[...] = dr
        drb_ref[...] = dr.astype(drb_ref.dtype)

        @pl.when(i == 0)
        def _():
            dg_ref[...] = jnp.zeros_like(dg_ref)
            db_ref[...] = jnp.zeros_like(db_ref)
            if meta_rows:
                refs[4][...] = jnp.zeros_like(refs[4])

        dg_ref[...] += jnp.sum(dy * xhat, axis=0, keepdims=True)
        db_ref[...] += jnp.sum(dy, axis=0, keepdims=True)
        if meta_rows:
            @pl.when(i % tiles_per_seq == 0)
            def _():
                refs[4][...] += dr[:meta_rows, :]

    row = pl.BlockSpec((tr, d), lambda i: (i, 0))
    vec = pl.BlockSpec((1, d), lambda i: (0, 0))
    args = [a] + ([b] if b is not None else []) + [gamma.reshape(1, d), dy1] + ([dy2] if dy2 is not None else [])
    in_specs = [row] + ([row] if b is not None else []) + [vec, row] + ([row] if dy2 is not None else [])
    out_shape = [jax.ShapeDtypeStruct((rows, d), F32), jax.ShapeDtypeStruct((rows, d), MM_DTYPE),
                 jax.ShapeDtypeStruct((1, d), F32), jax.ShapeDtypeStruct((1, d), F32)]
    out_specs = [row, row, vec, vec]
    if meta_rows:
        out_shape.append(jax.ShapeDtypeStruct((meta_rows, d), F32))
        out_specs.append(pl.BlockSpec((meta_rows, d), lambda i: (0, 0)))
    return pl.pallas_call(
        body, name=name, out_shape=tuple(out_shape), grid=(rows // tr,), in_specs=in_specs,
        out_specs=tuple(out_specs), compiler_params=_cparams("arbitrary"),
    )(*args)


def _loss(y, tgt, seq, tr, tiles_per_seq, name):
    rows, d = y.shape

    def body(y_ref, t_ref, loss_ref, dy_ref):
        i = pl.program_id(0)
        t_in_seq = (i % tiles_per_seq) * tr + lax.broadcasted_iota(jnp.int32, (tr, 1), 0)
        valid = (t_in_seq >= N_META) & (t_in_seq < N_META + seq)
        diff = jnp.where(valid, y_ref[...] - t_ref[...], 0.0)
        dy_ref[...] = diff * (1.0 / d)

        @pl.when(i == 0)
        def _():
            loss_ref[...] = jnp.zeros_like(loss_ref)

        loss_ref[...] += jnp.full(loss_ref.shape, 0.5 / d, F32) * jnp.sum(diff * diff)

    row = pl.BlockSpec((tr, d), lambda i: (i, 0))
    return pl.pallas_call(
        body, name=name,
        out_shape=(jax.ShapeDtypeStruct((1, LANE), F32), jax.ShapeDtypeStruct((rows, d), F32)),
        grid=(rows // tr,), in_specs=[row, row],
        out_specs=(pl.BlockSpec((1, LANE), lambda i: (0, 0)), row),
        compiler_params=_cparams("arbitrary"),
    )(y, tgt)


def _dot(a, b, dims):
    return lax.dot_general(a, b, (dims, ((), ())), preferred_element_type=F32)


def _split(x):
    hi = x.astype(SPLIT_DTYPE)
    lo = (x - hi.astype(F32)).astype(SPLIT_DTYPE)
    return hi, lo


def _dot3(ap, bp, dims):
    return _dot(ap[0], bp[0], dims) + (_dot(ap[0], bp[1], dims) + _dot(ap[1], bp[0], dims))


def _tri_sums(xs, tri):
    parts = []
    for x in xs:
        parts.extend(_split(x))
    out = _dot(jnp.concatenate(parts, axis=0), tri, ((1,), (0,)))
    n = xs[0].shape[0]
    return [out[2 * i * n:(2 * i + 1) * n] + out[(2 * i + 1) * n:(2 * i + 2) * n] for i in range(len(xs))]


def _split_wide(x):
    hi = x.astype(SPLIT_DTYPE).astype(F32)
    return hi, x - hi


def _lhs3(x, axis):
    hi, lo = _split_wide(x)
    return jnp.concatenate([hi, hi, lo], axis=axis).astype(SPLIT_DTYPE)


def _rhs3(x, axis):
    hi, lo = _split_wide(x)
    return jnp.concatenate([hi, lo, hi], axis=axis).astype(SPLIT_DTYPE)


def _tri_masks():
    row = lax.broadcasted_iota(jnp.int32, (QBLK, QBLK), 0)
    col = lax.broadcasted_iota(jnp.int32, (QBLK, QBLK), 1)
    return row, col


def _attn_fwd(u, n_seq, t_rows, cw, name):
    rows = u.shape[0]
    nb = t_rows // QBLK
    width = min(2 * cw, ATTN_LANES)
    n_hd = width // HEAD_DIM
    groups = (2 * cw) // width
    kcol, vcol = (2 * cw) // width, (4 * cw) // width

    def body(q_ref, k_ref, v_ref, o_ref):
        i = pl.program_id(2)
        row, col = _tri_masks()
        incl = (row >= col).astype(SPLIT_DTYPE)
        past = col < row
        sls = [slice(hd * HEAD_DIM, (hd + 1) * HEAD_DIM) for hd in range(n_hd)]
        q3s = [_lhs3(q_ref[:, sl] * (HEAD_DIM ** -0.5), 1) for sl in sls]

        def tile(j, carries, accs, masked):
            r0 = pl.multiple_of(j * QBLK, QBLK)
            zs = [_dot(q3s[hd], _rhs3(k_ref[pl.ds(r0, QBLK), sl], 1), ((1,), (1,))) for hd, sl in enumerate(sls)]
            lks = []
            for z in zs:
                lk = -(jnp.maximum(z, 0.0) + jnp.log1p(jnp.exp(-jnp.abs(z))))
                lks.append(jnp.where(past, lk, 0.0) if masked else lk)
            cls = _tri_sums(lks, incl)
            new_c, new_a = [], []
            for hd, sl in enumerate(sls):
                a = jnp.exp(zs[hd] + cls[hd] + carries[hd])
                if masked:
                    a = jnp.where(past, a, 0.0)
                v3 = _rhs3(v_ref[pl.ds(r0, QBLK), sl], 0)
                new_a.append(accs[hd] + _dot(_lhs3(a, 1), v3, ((1,), (0,))))
                new_c.append(carries[hd] + cls[hd][:, 0:1])
            return tuple(new_c), tuple(new_a)

        c0 = tuple(jnp.zeros((QBLK, 1), F32) for _ in sls)
        a0 = tuple(jnp.zeros((QBLK, HEAD_DIM), F32) for _ in sls)
        st = tile(i, c0, a0, True)
        st = lax.fori_loop(0, i, lambda jj, c: tile(i - 1 - jj, c[0], c[1], False), st)
        o_ref[...] = jnp.concatenate(st[1], axis=1)

    q_spec = pl.BlockSpec((QBLK, width), lambda s, p, i: (s * nb + i, p))
    k_spec = pl.BlockSpec((t_rows, width), lambda s, p, i: (s, kcol + p))
    v_spec = pl.BlockSpec((t_rows, width), lambda s, p, i: (s, vcol + p))
    return pl.pallas_call(
        body, name=name,
        out_shape=jax.ShapeDtypeStruct((rows, 2 * cw), F32),
        grid=(n_seq, groups, nb), in_specs=[q_spec, k_spec, v_spec],
        out_specs=pl.BlockSpec((QBLK, width), lambda s, p, i: (s * nb + i, p)),
        compiler_params=_cparams("parallel", "parallel", "arbitrary"),
    )(u, u, u)


def _attn_bwd(u, o, do, n_seq, t_rows, cw, name):
    rows = u.shape[0]
    nb = t_rows // QBLK
    width = min(2 * cw, ATTN_LANES)
    n_hd = width // HEAD_DIM
    groups = (2 * cw) // width
    kcol, vcol = (2 * cw) // width, (4 * cw) // width

    def body(q_ref, k_ref, v_ref, o_ref, do_ref, dq_ref, dk_ref, dv_ref, dk_acc, dv_acc):
        i = pl.program_id(2)
        row, col = _tri_masks()
        incl = (row >= col).astype(SPLIT_DTYPE)
        strict = (row > col).astype(SPLIT_DTYPE)
        past = col < row

        @pl.when(i == 0)
        def _():
            dk_acc[...] = jnp.zeros_like(dk_acc)
            dv_acc[...] = jnp.zeros_like(dv_acc)

        sls = [slice(hd * HEAD_DIM, (hd + 1) * HEAD_DIM) for hd in range(n_hd)]
        qs = [q_ref[:, sl] * (HEAD_DIM ** -0.5) for sl in sls]
        q3s = [_lhs3(q, 1) for q in qs]
        q3r = [_rhs3(q, 0) for q in qs]
        do3s = [_lhs3(do_ref[:, sl], 1) for sl in sls]
        do3r = [_rhs3(do_ref[:, sl], 0) for sl in sls]
        dtots = [jnp.sum(do_ref[:, sl] * o_ref[:, sl], axis=-1, keepdims=True) for sl in sls]

        def tile(j, ccs, cgs, dqs, masked):
            r0 = pl.multiple_of(j * QBLK, QBLK)
            ks = [k_ref[pl.ds(r0, QBLK), sl] for sl in sls]
            vs = [v_ref[pl.ds(r0, QBLK), sl] for sl in sls]
            zs = [_dot(q3s[hd], _rhs3(ks[hd], 1), ((1,), (1,))) for hd in range(n_hd)]
            das = [_dot(do3s[hd], _rhs3(vs[hd], 1), ((1,), (1,))) for hd in range(n_hd)]
            lks, sigs = [], []
            for z in zs:
                e = jnp.exp(-jnp.abs(z))
                lk = -(jnp.maximum(z, 0.0) + jnp.log1p(e))
                sigs.append(jnp.where(z >= 0.0, 1.0, e) / (1.0 + e))
                lks.append(jnp.where(past, lk, 0.0) if masked else lk)
            cls = _tri_sums(lks, incl)
            a_s, gs = [], []
            for hd in range(n_hd):
                a = jnp.exp(zs[hd] + cls[hd] + ccs[hd])
                if masked:
                    a = jnp.where(past, a, 0.0)
                a_s.append(a)
                gs.append(a * das[hd])
            gxs = _tri_sums(gs, strict)
            n_cc, n_cg, n_dq = [], [], []
            for hd in range(n_hd):
                dz = gs[hd] - sigs[hd] * (dtots[hd] - cgs[hd] - gxs[hd])
                if masked:
                    dz = jnp.where(past, dz, 0.0)
                n_dq.append(dqs[hd] + _dot(_lhs3(dz, 1), _rhs3(ks[hd], 0), ((1,), (0,))))
                dk_acc[hd, pl.ds(r0, QBLK), :] += _dot(_lhs3(dz, 0), q3r[hd], ((0,), (0,)))
                dv_acc[hd, pl.ds(r0, QBLK), :] += _dot(_lhs3(a_s[hd], 0), do3r[hd], ((0,), (0,)))
                n_cc.append(ccs[hd] + cls[hd][:, 0:1])
                n_cg.append(cgs[hd] + gxs[hd][:, 0:1] + gs[hd][:, 0:1])
            return tuple(n_cc), tuple(n_cg), tuple(n_dq)

        z1 = tuple(jnp.zeros((QBLK, 1), F32) for _ in sls)
        dq0 = tuple(jnp.zeros((QBLK, HEAD_DIM), F32) for _ in sls)
        st = tile(i, z1, z1, dq0, True)
        st = lax.fori_loop(0, i, lambda jj, c: tile(i - 1 - jj, c[0], c[1], c[2], False), st)
        dq_ref[...] = (jnp.concatenate(st[2], axis=1) * (HEAD_DIM ** -0.5)).astype(dq_ref.dtype)

        @pl.when(i == nb - 1)
        def _():
            dk_ref[...] = jnp.concatenate([dk_acc[h] for h in range(n_hd)], axis=1).astype(dk_ref.dtype)
            dv_ref[...] = jnp.concatenate([dv_acc[h] for h in range(n_hd)], axis=1).astype(dv_ref.dtype)

    blk = pl.BlockSpec((QBLK, width), lambda s, p, i: (s * nb + i, p))
    k_spec = pl.BlockSpec((t_rows, width), lambda s, p, i: (s, kcol + p))
    v_spec = pl.BlockSpec((t_rows, width), lambda s, p, i: (s, vcol + p))
    seq_blk = pl.BlockSpec((t_rows, width), lambda s, p, i: (s, p))
    shp = jax.ShapeDtypeStruct((rows, 2 * cw), MM_DTYPE)
    return pl.pallas_call(
        body, name=name, out_shape=(shp, shp, shp),
        grid=(n_seq, groups, nb), in_specs=[blk, k_spec, v_spec, blk, blk],
        out_specs=(blk, seq_blk, seq_blk),
        scratch_shapes=[pltpu.VMEM((n_hd, t_rows, HEAD_DIM), F32), pltpu.VMEM((n_hd, t_rows, HEAD_DIM), F32)],
        compiler_params=_cparams("parallel", "parallel", "arbitrary"),
    )(u, u, u, o, do)


def _shifted(xw, shift):
    n = xw.shape[0]
    s = shift % n
    return xw if s == 0 else pltpu.roll(xw, s, 0)


def _causal_conv(xw, w_ref, kw, shift_sign=1):
    acc = None
    for kk in range(kw):
        term = w_ref[kk:kk + 1, :] * _shifted(xw, shift_sign * (kw - 1 - kk))
        acc = term if acc is None else acc + term
    return acc


def _rms(o, g):
    r = lax.rsqrt(jnp.mean(o * o, axis=-1, keepdims=True) + RMS_EPS)
    return o * r * g, r


def _rms_bwd(o, g, dy):
    r = lax.rsqrt(jnp.mean(o * o, axis=-1, keepdims=True) + RMS_EPS)
    gdy = g * dy
    do = gdy * r - o * (r * r * r) * jnp.mean(gdy * o, axis=-1, keepdims=True)
    return do, dy * o * r


def _mix_specs(rows, cw, halo_prev, halo_next):
    tr = CONV_ROWS
    per = tr // HALO
    last = rows // HALO - 1

    def group(cb):
        specs = []
        if halo_prev:
            specs.append(pl.BlockSpec((HALO, cw), lambda i: (jnp.maximum(i * per - 1, 0), cb)))
        specs.append(pl.BlockSpec((tr, cw), lambda i: (i, cb)))
        if halo_next:
            specs.append(pl.BlockSpec((HALO, cw), lambda i: (jnp.minimum((i + 1) * per, last), cb)))
        return specs
    return group


def _window(prev_ref, cur_ref, first):
    return jnp.concatenate([jnp.where(first, 0.0, prev_ref[...]), cur_ref[...]], axis=0)


def _mix_fwd(u, o_sb, w31, b31, lng, lnb, w3, gmix, t_rows, cw, name):
    rows = u.shape[0]
    d = 4 * cw
    tr = CONV_ROWS
    tiles_per_seq = t_rows // tr
    k31, k3 = w31.shape[0], w3.shape[0]
    group = _mix_specs(rows, cw, True, False)

    def body(ap, ac, gp, gc, bc, cp, cc, hp, hc, osb_ref, w31_ref, b31_ref, lng_ref, lnb_ref, w3_ref, gm_ref, y_ref):
        first = (pl.program_id(0) % tiles_per_seq) == 0
        aw = _window(ap, ac, first)
        gw = _window(gp, gc, first)
        hcw = aw * jax.nn.sigmoid(gw)
        yc = (_causal_conv(hcw, w31_ref, k31) + b31_ref[...])[HALO:]
        mu = jnp.mean(yc, axis=-1, keepdims=True)
        xc = yc - mu
        var = jnp.mean(xc * xc, axis=-1, keepdims=True)
        lc = xc * lax.rsqrt(var + LN_EPS) * lng_ref[...] + lnb_ref[...]
        oc = lc * jax.nn.sigmoid(lc)
        pw = _window(cp, cc, first) * _window(hp, hc, first)
        osc = bc[...] * _causal_conv(pw, w3_ref, k3)[HALO:]
        gm = gm_ref[...]
        y_sb, _ = _rms(osb_ref[...], gm[:, :2 * cw])
        y_c, _ = _rms(oc, gm[:, 2 * cw:3 * cw])
        y_s, _ = _rms(osc, gm[:, 3 * cw:])
        y_ref[...] = jnp.concatenate([y_sb, y_c, y_s], axis=1).astype(y_ref.dtype)

    full = lambda arr: pl.BlockSpec(arr.shape, lambda i: (0, 0))
    small = [w31, b31.reshape(1, cw), lng.reshape(1, cw), lnb.reshape(1, cw), w3, gmix.reshape(1, d)]
    in_specs = (group(6) + group(7) + [pl.BlockSpec((tr, cw), lambda i: (i, 8))] + group(9) + group(10)
                + [pl.BlockSpec((tr, 2 * cw), lambda i: (i, 0))] + [full(s) for s in small])
    return pl.pallas_call(
        body, name=name, out_shape=jax.ShapeDtypeStruct((rows, d), MM_DTYPE),
        grid=(rows // tr,), in_specs=in_specs, out_specs=pl.BlockSpec((tr, d), lambda i: (i, 0)),
        compiler_params=_cparams("parallel"),
    )(u, u, u, u, u, u, u, u, u, o_sb, *small)


def _mix_bwd(u, o_sb, dy, w31, b31, lng, lnb, w3, gmix, t_rows, cw, name):
    rows = u.shape[0]
    d = 4 * cw
    tr = CONV_ROWS
    tiles_per_seq = t_rows // tr
    k31, k3 = w31.shape[0], w3.shape[0]
    group = _mix_specs(rows, cw, True, True)
    cur = slice(HALO, HALO + tr)

    def body(*refs):
        refs = list(refs)
        take = lambda n: [refs.pop(0) for _ in range(n)]
        a3, g3, b3, c3, h3, dyc3, dys3 = (take(3) for _ in range(7))
        osb_ref, dysb_ref, w31_ref, b31_ref, lng_ref, lnb_ref, w3_ref, gm_ref = take(8)
        dosb_ref, du_ref, dgm_ref, dw31_ref, db31_ref, dlng_ref, dlnb_ref, dw3_ref = refs
        i = pl.program_id(0)
        first = (i % tiles_per_seq) == 0
        last = (i % tiles_per_seq) == tiles_per_seq - 1
        wrow = lax.broadcasted_iota(jnp.int32, (tr + 2 * HALO, 1), 0)
        keep = jnp.logical_not((first & (wrow < HALO)) | (last & (wrow >= HALO + tr)))

        def win(r3):
            return jnp.concatenate([r3[0][...], r3[1][...], r3[2][...]], axis=0)

        gm = gm_ref[...]
        aw, gw, bw, cwin, hw = win(a3), win(g3), win(b3), win(c3), win(h3)
        dy_c = jnp.where(keep, win(dyc3), 0.0)
        dy_s = jnp.where(keep, win(dys3), 0.0)
        sg = jax.nn.sigmoid(gw)
        hcw = jnp.where(keep, aw * sg, 0.0)
        yc = _causal_conv(hcw, w31_ref, k31) + b31_ref[...]
        mu = jnp.mean(yc, axis=-1, keepdims=True)
        xc = yc - mu
        rstd = lax.rsqrt(jnp.mean(xc * xc, axis=-1, keepdims=True) + LN_EPS)
        nrm = xc * rstd
        lc = nrm * lng_ref[...] + lnb_ref[...]
        sl = jax.nn.sigmoid(lc)
        oc = lc * sl
        d_oc, dg_c = _rms_bwd(oc, gm[:, 2 * cw:3 * cw], dy_c)
        dlc = d_oc * sl * (1.0 + lc * (1.0 - sl))
        dn = dlc * lng_ref[...]
        dyc = rstd * (dn - jnp.mean(dn, axis=-1, keepdims=True) - nrm * jnp.mean(dn * nrm, axis=-1, keepdims=True))
        dyc = jnp.where(keep, dyc, 0.0)
        dhc = _causal_conv(dyc, w31_ref, k31, -1)
        da = dhc * sg
        dgate = dhc * aw * sg * (1.0 - sg)
        pw = jnp.where(keep, cwin * hw, 0.0)
        cs = _causal_conv(pw, w3_ref, k3)
        osc = bw * cs
        d_osc, dg_s = _rms_bwd(osc, gm[:, 3 * cw:], dy_s)
        dcs = jnp.where(keep, d_osc * bw, 0.0)
        dbg = d_osc * cs
        dp = _causal_conv(dcs, w3_ref, k3, -1)
        dcg = dp * hw
        dhs = dp * cwin
        du_ref[...] = jnp.concatenate([x[cur] for x in (da, dgate, dbg, dcg, dhs)], axis=1).astype(du_ref.dtype)
        d_osb, dg_sb = _rms_bwd(osb_ref[...], gm[:, :2 * cw], dysb_ref[...])
        dosb_ref[...] = d_osb

        @pl.when(i == 0)
        def _():
            for r in (dgm_ref, dw31_ref, db31_ref, dlng_ref, dlnb_ref, dw3_ref):
                r[...] = jnp.zeros_like(r)

        csum = lambda x: jnp.sum(x, axis=0, keepdims=True)
        dgm_ref[...] += jnp.concatenate([csum(dg_sb), csum(dg_c[cur]), csum(dg_s[cur])], axis=1)
        db31_ref[...] += csum(dyc[cur])
        dlng_ref[...] += csum((dlc * nrm)[cur])
        dlnb_ref[...] += csum(dlc[cur])
        for kk in range(k31):
            dw31_ref[kk:kk + 1, :] += csum((dyc * _shifted(hcw, k31 - 1 - kk))[cur])
        for kk in range(k3):
            dw3_ref[kk:kk + 1, :] += csum((dcs * _shifted(pw, k3 - 1 - kk))[cur])

    full = lambda arr: pl.BlockSpec(arr.shape, lambda i: (0, 0))
    small = [w31, b31.reshape(1, cw), lng.reshape(1, cw), lnb.reshape(1, cw), w3, gmix.reshape(1, d)]
    in_specs = (group(6) + group(7) + group(8) + group(9) + group(10) + group(2) + group(3)
                + [pl.BlockSpec((tr, 2 * cw), lambda i: (i, 0))] * 2 + [full(s) for s in small])
    acc = lambda r, c: (jax.ShapeDtypeStruct((r, c), F32), pl.BlockSpec((r, c), lambda i: (0, 0)))
    outs = [(jax.ShapeDtypeStruct((rows, 2 * cw), F32), pl.BlockSpec((tr, 2 * cw), lambda i: (i, 0))),
            (jax.ShapeDtypeStruct((rows, 5 * cw), MM_DTYPE), pl.BlockSpec((tr, 5 * cw), lambda i: (i, 0))),
            acc(1, d), acc(k31, cw), acc(1, cw), acc(1, cw), acc(1, cw), acc(k3, cw)]
    return pl.pallas_call(
        body, name=name, out_shape=tuple(o[0] for o in outs), grid=(rows // tr,), in_specs=in_specs,
        out_specs=tuple(o[1] for o in outs), compiler_params=_cparams("arbitrary"),
    )(*([u] * 15), *([dy] * 6), o_sb, dy, *small)


def _position():
    return lax.axis_index("x"), lax.axis_index("y"), lax.axis_index("c")


def _all_gather(x, name):
    def body(x_ref, out_ref, send_sems, recv_sems, local_sem):
        mx, my, mc = _position()
        me, sibling = (mx, my, mc), (mx, my, 1 - mc)
        chips = [(1 - mx, my), (mx, 1 - my), (1 - mx, 1 - my)]

        def slot(px, py, pc):
            return out_ref.at[4 * px + 2 * py + pc]

        def copy(k, block, to, src=None):
            return pltpu.make_async_remote_copy(
                src_ref=slot(*block) if src is None else src, dst_ref=slot(*block),
                send_sem=send_sems.at[k], recv_sem=recv_sems.at[k], device_id=to, device_id_type=MESH_IDS)

        mine = pltpu.make_async_copy(x_ref, slot(*me), local_sem)
        mine.start()
        first = [copy(0, me, sibling, src=x_ref)]
        first += [copy(1 + j, me, (*chip, mc), src=x_ref) for j, chip in enumerate(chips)]
        for cp in first:
            cp.start()
        passed = [copy(4 + j, (*chip, mc), sibling) for j, chip in enumerate(chips)]
        for j, chip in enumerate(chips):
            copy(1 + j, (*chip, mc), me).wait_recv()
            passed[j].start()
        copy(0, sibling, me).wait_recv()
        for j, chip in enumerate(chips):
            copy(4 + j, (*chip, 1 - mc), me).wait_recv()
        for cp in first + passed:
            cp.wait_send()
        mine.wait()

    return pl.pallas_call(
        body, name=name, out_shape=jax.ShapeDtypeStruct((N_DEV,) + x.shape, x.dtype),
        in_specs=[pl.BlockSpec(memory_space=pl.ANY)], out_specs=pl.BlockSpec(memory_space=pl.ANY),
        scratch_shapes=[pltpu.SemaphoreType.DMA((N_DEV - 1,)), pltpu.SemaphoreType.DMA((N_DEV - 1,)),
                        pltpu.SemaphoreType.DMA],
    )(x)


def _all_to_all(sends, name):
    n_w = len(sends)

    def body(*refs):
        send_refs, recv_refs = refs[:n_w], refs[n_w:2 * n_w]
        send_sems, recv_sems = refs[2 * n_w:]
        mx, my, mc = _position()
        copies = []
        for k in range(1, N_DEV):
            px, py, pc = mx ^ (k >> 2), my ^ ((k >> 1) & 1), mc ^ (k & 1)
            for w in range(n_w):
                copies.append(pltpu.make_async_remote_copy(
                    src_ref=send_refs[w].at[4 * px + 2 * py + pc], dst_ref=recv_refs[w].at[k - 1],
                    send_sem=send_sems.at[w * (N_DEV - 1) + k - 1], recv_sem=recv_sems.at[w * (N_DEV - 1) + k - 1],
                    device_id=(px, py, pc), device_id_type=MESH_IDS))
        for cp in copies:
            cp.start()
        for cp in copies:
            cp.wait_recv()
        for cp in copies:
            cp.wait_send()

    hbm = pl.BlockSpec(memory_space=pl.ANY)
    n_sem = n_w * (N_DEV - 1)
    return pl.pallas_call(
        body, name=name,
        out_shape=tuple(jax.ShapeDtypeStruct((N_DEV - 1,) + s.shape[1:], s.dtype) for s in sends),
        in_specs=[hbm] * n_w, out_specs=tuple([hbm] * n_w),
        scratch_shapes=[pltpu.SemaphoreType.DMA((n_sem,)), pltpu.SemaphoreType.DMA((n_sem,))],
    )(*sends)


def _adam_math(w, g, m, v):
    m = ADAM_B1 * m + (1.0 - ADAM_B1) * g
    v = ADAM_B2 * v + (1.0 - ADAM_B2) * (g * g)
    m_hat = m / (1.0 - ADAM_B1 ** ADAM_STEP)
    v_hat = v / (1.0 - ADAM_B2 ** ADAM_STEP)
    delta = -ADAM_LR * (m_hat / (jnp.sqrt(v_hat) + ADAM_EPS) + ADAM_WD * w)
    return delta, m, v


def _adam(g_own, recv, w, m, v, name):
    rows, cols = w.shape
    tr = _tile(rows, max(8, ADAM_BLOCK_ELEMS // cols), 8)

    def body(*refs):
        if recv is not None:
            g_ref, r_ref, w_ref, m_ref, v_ref, go_ref, d_ref, mo_ref, vo_ref = refs
            g = g_ref[...]
            for k in range(recv.shape[0]):
                g = g + r_ref[k].astype(F32)
        else:
            g_ref, w_ref, m_ref, v_ref, go_ref, d_ref, mo_ref, vo_ref = refs
            g = g_ref[...]
        delta, mn, vn = _adam_math(w_ref[...], g, m_ref[...], v_ref[...])
        go_ref[...] = g
        d_ref[...] = delta
        mo_ref[...] = mn
        vo_ref[...] = vn

    blk = pl.BlockSpec((tr, cols), lambda i: (i, 0))
    in_specs = [blk] + ([pl.BlockSpec((recv.shape[0], tr, cols), lambda i: (0, i, 0))] if recv is not None else []) + [blk] * 3
    args = [g_own] + ([recv] if recv is not None else []) + [w, m, v]
    shp = jax.ShapeDtypeStruct((rows, cols), F32)
    return pl.pallas_call(
        body, name=name, out_shape=(shp,) * 4, grid=(rows // tr,), in_specs=in_specs, out_specs=(blk,) * 4,
        compiler_params=_cparams("parallel"),
    )(*args)


def _sum_devices(parts, name):
    n, rows, cols = parts.shape

    def body(p_ref, o_ref):
        acc = p_ref[0]
        for k in range(1, n):
            acc = acc + p_ref[k]
        o_ref[...] = acc

    return pl.pallas_call(
        body, name=name, out_shape=jax.ShapeDtypeStruct((rows, cols), F32),
        in_specs=[pl.BlockSpec(memory_space=pltpu.VMEM)], out_specs=pl.BlockSpec(memory_space=pltpu.VMEM),
    )(parts)


def _pack(arrs, dtype):
    flat = jnp.concatenate([a.reshape(-1).astype(dtype) for a in arrs])
    pad = (-flat.shape[0]) % (16 * LANE)
    return jnp.pad(flat, (0, pad)).reshape(-1, LANE)


def _unpack(flat2d, shapes):
    flat = flat2d.reshape(-1)
    out, off = [], 0
    for s in shapes:
        n = 1
        for dim in s:
            n *= dim
        out.append(flat[off:off + n].reshape(s))
        off += n
    return out


def _unpack_gathered(g, shapes):
    flat = g.reshape(N_DEV, -1)
    out, off = [], 0
    for s in shapes:
        n = 1
        for dim in s:
            n *= dim
        out.append(flat[:, off:off + n].reshape((N_DEV,) + tuple(s)))
        off += n
    return out


def _cols_from_devices(g):
    nd = g.ndim
    perm = tuple(range(1, nd - 1)) + (0, nd - 1)
    t = jnp.transpose(g, perm)
    return t.reshape(t.shape[:-2] + (t.shape[-2] * t.shape[-1],))


def _cols_to_devices(a, dtype):
    c = a.shape[-1] // N_DEV
    t = a.reshape(a.shape[:-1] + (N_DEV, c)).astype(dtype)
    nd = t.ndim
    return jnp.transpose(t, (nd - 2,) + tuple(range(nd - 2)) + (nd - 1,))


def _rows_to_devices(a, dtype):
    nl, r8, c = a.shape
    return jnp.transpose(a.reshape(nl, N_DEV, r8 // N_DEV, c).astype(dtype), (1, 0, 2, 3))


def kernel(x, meta_tokens, ln_in_g, ln_in_b, w_in, w_conf_dw, b_conf_dw, ln_conf_g, ln_conf_b, w_short_dw, g_mix, w_out, ln_mix_g, ln_mix_b, w_ff1, w_ff2, ln_ff_g, ln_ff_b, loss_target, m_meta_tokens, m_ln_in_g, m_ln_in_b, m_w_in, m_w_conf_dw, m_b_conf_dw, m_ln_conf_g, m_ln_conf_b, m_w_short_dw, m_g_mix, m_w_out, m_ln_mix_g, m_ln_mix_b, m_w_ff1, m_w_ff2, m_ln_ff_g, m_ln_ff_b, v_meta_tokens, v_ln_in_g, v_ln_in_b, v_w_in, v_w_conf_dw, v_b_conf_dw, v_ln_conf_g, v_ln_conf_b, v_w_short_dw, v_g_mix, v_w_out, v_ln_mix_g, v_ln_mix_b, v_w_ff1, v_w_ff2, v_ln_ff_g, v_ln_ff_b):
    n_seq, seq, d = x.shape
    depth = w_in.shape[0]
    cw = d // 4
    alpha = (2.0 * depth) ** 0.25
    t_rows = -(-(N_META + seq) // QBLK) * QBLK
    rows = n_seq * t_rows
    tr = _row_tile(t_rows)
    tiles_per_seq = t_rows // tr
    me = 4 * lax.axis_index("x") + 2 * lax.axis_index("y") + lax.axis_index("c")

    small_shards = [meta_tokens, w_conf_dw, w_short_dw]
    sm = _unpack_gathered(_all_gather(_pack(small_shards, F32), "gather_small_params"), [a.shape for a in small_shards])
    meta_full, w31_full, w3_full = (_cols_from_devices(a) for a in sm)
    big_shards = [w_in, w_out, w_ff1, w_ff2]
    bg = _unpack_gathered(_all_gather(_pack(big_shards, MM_DTYPE), "gather_weights"), [a.shape for a in big_shards])
    win_f = _cols_from_devices(bg[0])
    wout_f = jnp.transpose(bg[1], (1, 0, 2, 3)).reshape(depth, d, d)
    wff1_f = _cols_from_devices(bg[2])
    wff2_f = jnp.transpose(bg[3], (1, 0, 2, 3)).reshape(depth, 4 * d, d)

    pad_rows = t_rows - N_META - seq
    xin = jnp.concatenate([jnp.broadcast_to(meta_full[None], (n_seq, N_META, d)), x,
                           jnp.zeros((n_seq, pad_rows, d), F32)], axis=1).reshape(rows, d)
    tgt = jnp.pad(loss_target, ((0, 0), (N_META, pad_rows), (0, 0))).reshape(rows, d)
    h, hb = _ln_fwd(xin, None, 1.0, ln_in_g, ln_in_b, tr, "ln_in")
    saved = []
    for l in range(depth):
        u = _matmul(hb, win_f, "nn", F32, "proj_in", layer=l, pn=1408)
        o_sb = _attn_fwd(u, n_seq, t_rows, cw, "attn_fwd")
        yb = _mix_fwd(u, o_sb, w31_full[l], b_conf_dw[l], ln_conf_g[l], ln_conf_b[l], w3_full[l], g_mix[l],
                      t_rows, cw, "mix_fwd")
        mix = _matmul(yb, wout_f, "nn", F32, "proj_out", layer=l)
        h1, h1b = _ln_fwd(h, mix, alpha, ln_mix_g[l], ln_mix_b[l], tr, "ln_mix")
        act = _matmul(h1b, wff1_f, "nn", MM_DTYPE, "ff1", layer=l, epilogue="relu2")
        ff = _matmul(act, wff2_f, "nn", F32, "ff2", layer=l)
        h2, h2b = _ln_fwd(h1, ff, alpha, ln_ff_g[l], ln_ff_b[l], tr, "ln_ff")
        saved.append((h, hb, u, o_sb, yb, mix, h1, h1b, act, ff))
        h, hb = h2, h2b
    loss_part, dy1 = _loss(h, tgt, seq, tr, tiles_per_seq, "loss")
    loss = lax.psum(loss_part[0, 0], MESH_AXES)

    dy2, a_dy = None, 1.0
    gw_in, gw_out, gw_ff1, gw_ff2 = [], [], [], []
    g_small = {n: [] for n in ("w31", "b31", "lng", "lnb", "w3", "gmix", "lmg", "lmb", "lfg", "lfb")}
    for l in reversed(range(depth)):
        h, hb, u, o_sb, yb, mix, h1, h1b, act, ff = saved[l]
        dr2, dr2b, dg, db = _ln_bwd(h1, ff, alpha, ln_ff_g[l], dy1, dy2, a_dy, tr, "ln_ff_bwd")
        g_small["lfg"].append(dg)
        g_small["lfb"].append(db)
        df1b = _matmul(dr2b, wff2_f, "nt", MM_DTYPE, "ff2_dx", layer=l, epilogue="dsqrelu", extra=act)
        gw_ff2.append(_matmul(act, dr2b, "tn", F32, "ff2_dw", pm=1024, pk=544))
        dh1 = _matmul(df1b, wff1_f, "nt", F32, "ff1_dx", layer=l)
        gw_ff1.append(_matmul(h1b, df1b, "tn", F32, "ff1_dw", pm=1024, pk=544))
        dr1, dr1b, dg, db = _ln_bwd(h, mix, alpha, ln_mix_g[l], dr2, dh1, alpha, tr, "ln_mix_bwd")
        g_small["lmg"].append(dg)
        g_small["lmb"].append(db)
        dyr = _matmul(dr1b, wout_f, "nt", F32, "proj_out_dx", layer=l)
        gw_out.append(_matmul(yb, dr1b, "tn", F32, "proj_out_dw", pm=1024, pk=544))
        d_osb, du_conv, dgm, dw31, db31, dlng, dlnb, dw3 = _mix_bwd(
            u, o_sb, dyr, w31_full[l], b_conf_dw[l], ln_conf_g[l], ln_conf_b[l], w3_full[l], g_mix[l],
            t_rows, cw, "mix_bwd")
        for n, val in zip(("gmix", "w31", "b31", "lng", "lnb", "w3"), (dgm, dw31, db31, dlng, dlnb, dw3)):
            g_small[n].append(val)
        dq, dk, dv = _attn_bwd(u, o_sb, d_osb, n_seq, t_rows, cw, "attn_bwd")
        du = jnp.concatenate([dq, dk, dv, du_conv], axis=1)
        dh = _matmul(du, win_f, "nt", F32, "proj_in_dx", layer=l, pk=1408)
        gw_in.append(_matmul(hb, du, "tn", F32, "proj_in_dw", pm=1024, pn=1408, pk=544))
        dy1, dy2, a_dy = dr1, dh, alpha
    dxin, _, dg_in, db_in, dmeta = _ln_bwd(xin, None, 1.0, ln_in_g, dy1, dy2, a_dy, tr, "ln_in_bwd",
                                           meta_rows=N_META, tiles_per_seq=tiles_per_seq)
    grad_x = dxin.reshape(n_seq, t_rows, d)[:, N_META:N_META + seq]

    stack = lambda lst: jnp.stack(lst[::-1])
    gw_in, gw_out, gw_ff1, gw_ff2 = stack(gw_in), stack(gw_out), stack(gw_ff1), stack(gw_ff2)
    sends = [_cols_to_devices(gw_in, COMM_GRAD_DTYPE), _rows_to_devices(gw_out, COMM_GRAD_DTYPE),
             _cols_to_devices(gw_ff1, COMM_GRAD_DTYPE), _rows_to_devices(gw_ff2, COMM_GRAD_DTYPE)]
    recvs = _all_to_all(sends, "exchange_weight_grads")
    own = [lax.dynamic_slice_in_dim(gw_in, me * w_in.shape[2], w_in.shape[2], 2),
           lax.dynamic_slice_in_dim(gw_out, me * w_out.shape[1], w_out.shape[1], 1),
           lax.dynamic_slice_in_dim(gw_ff1, me * w_ff1.shape[2], w_ff1.shape[2], 2),
           lax.dynamic_slice_in_dim(gw_ff2, me * w_ff2.shape[1], w_ff2.shape[1], 1)]
    big = {}
    for nm, g_own, rv, w, m, v in zip(("w_in", "w_out", "w_ff1", "w_ff2"), own, recvs,
                                      (w_in, w_out, w_ff1, w_ff2), (m_w_in, m_w_out, m_w_ff1, m_w_ff2),
                                      (v_w_in, v_w_out, v_w_ff1, v_w_ff2)):
        c = w.shape[-1]
        res = _adam(g_own.reshape(-1, c), rv.reshape(N_DEV - 1, -1, c), w.reshape(-1, c), m.reshape(-1, c),
                    v.reshape(-1, c), "adam_" + nm)
        big[nm] = [r.reshape(w.shape) for r in res]

    sstack = lambda n: jnp.stack(g_small[n][::-1])
    partial = {
        "meta_tokens": dmeta, "ln_in_g": dg_in.reshape(d), "ln_in_b": db_in.reshape(d),
        "w_conf_dw": sstack("w31"), "b_conf_dw": sstack("b31").reshape(depth, cw),
        "ln_conf_g": sstack("lng").reshape(depth, cw), "ln_conf_b": sstack("lnb").reshape(depth, cw),
        "w_short_dw": sstack("w3"), "g_mix": sstack("gmix").reshape(depth, d),
        "ln_mix_g": sstack("lmg").reshape(depth, d), "ln_mix_b": sstack("lmb").reshape(depth, d),
        "ln_ff_g": sstack("lfg").reshape(depth, d), "ln_ff_b": sstack("lfb").reshape(depth, d),
    }
    names = list(partial)
    gathered = _all_gather(_pack([partial[n] for n in names], F32), "gather_small_grads")
    summed = _unpack(_sum_devices(gathered, "sum_small_grads"), [partial[n].shape for n in names])
    g_full = dict(zip(names, summed))
    local = {"meta_tokens": (meta_tokens, m_meta_tokens, v_meta_tokens), "ln_in_g": (ln_in_g, m_ln_in_g, v_ln_in_g),
             "ln_in_b": (ln_in_b, m_ln_in_b, v_ln_in_b), "w_conf_dw": (w_conf_dw, m_w_conf_dw, v_w_conf_dw),
             "b_conf_dw": (b_conf_dw, m_b_conf_dw, v_b_conf_dw), "ln_conf_g": (ln_conf_g, m_ln_conf_g, v_ln_conf_g),
             "ln_conf_b": (ln_conf_b, m_ln_conf_b, v_ln_conf_b), "w_short_dw": (w_short_dw, m_w_short_dw, v_w_short_dw),
             "g_mix": (g_mix, m_g_mix, v_g_mix), "ln_mix_g": (ln_mix_g, m_ln_mix_g, v_ln_mix_g),
             "ln_mix_b": (ln_mix_b, m_ln_mix_b, v_ln_mix_b), "ln_ff_g": (ln_ff_g, m_ln_ff_g, v_ln_ff_g),
             "ln_ff_b": (ln_ff_b, m_ln_ff_b, v_ln_ff_b)}
    g_loc = {}
    for n in names:
        wloc = local[n][0]
        g = g_full[n]
        if g.shape != wloc.shape:
            g = lax.dynamic_slice_in_dim(g, me * wloc.shape[-1], wloc.shape[-1], g.ndim - 1)
        g_loc[n] = g
    shapes = [g_loc[n].shape for n in names]
    _, sd, smo, svo = _adam(_pack([g_loc[n] for n in names], F32), None, _pack([local[n][0] for n in names], F32),
                            _pack([local[n][1] for n in names], F32), _pack([local[n][2] for n in names], F32),
                            "adam_small")
    small_d, small_m, small_v = (dict(zip(names, _unpack(a, shapes))) for a in (sd, smo, svo))

    order = ["meta_tokens", "ln_in_g", "ln_in_b", "w_in", "w_conf_dw", "b_conf_dw", "ln_conf_g", "ln_conf_b",
             "w_short_dw", "g_mix", "w_out", "ln_mix_g", "ln_mix_b", "w_ff1", "w_ff2", "ln_ff_g", "ln_ff_b"]
    grads = [big[n][0] if n in big else g_loc[n] for n in order]
    deltas = [big[n][1] if n in big else small_d[n] for n in order]
    new_m = [big[n][2] if n in big else small_m[n] for n in order]
    new_v = [big[n][3] if n in big else small_v[n] for n in order]
    return (loss, grad_x, *grads, *deltas, *new_m, *new_v)
```

```python
import functools

import jax
import jax.numpy as jnp
from jax import lax
from jax.experimental import pallas as pl
from jax.experimental.pallas import tpu as pltpu

F32 = jnp.float32
MM_DTYPE = jnp.float32
SPLIT_DTYPE = jnp.bfloat16
COMM_GRAD_DTYPE = jnp.bfloat16
N_DEV = 8
N_META = 16
HEAD_DIM = 64
LANE = 128
QBLK = 128
ATTN_LANES = 256
HALO = 32
CONV_ROWS = 128
ADAM_BLOCK_ELEMS = 128 * 1024
LN_EPS = 1e-5
RMS_EPS = 1e-6
ADAM_LR = 0.001
ADAM_B1 = 0.9
ADAM_B2 = 0.999
ADAM_EPS = 1e-08
ADAM_WD = 0.01
ADAM_STEP = 10
VMEM_LIMIT_BYTES = 48 * 1024 * 1024
MESH_AXES = ("x", "y", "c")
MESH_IDS = pl.DeviceIdType.MESH


def _cparams(*sem):
    return pltpu.CompilerParams(dimension_semantics=tuple(sem), vmem_limit_bytes=VMEM_LIMIT_BYTES)


def _tile(n, pref, mult=LANE):
    best = None
    t = mult
    while t <= min(n, pref):
        if n % t == 0:
            best = t
        t += mult
    return best if best is not None else n


def _row_tile(t_rows):
    for cand in (272, 256, 128):
        if t_rows % cand == 0:
            return cand
    raise ValueError(t_rows)


def _matmul(a, b, mode, out_dtype, name, layer=None, epilogue=None, extra=None, pm=544, pn=1024, pk=1024):
    if mode == "nn":
        m, k = a.shape
        n = b.shape[-1]
    elif mode == "nt":
        m, k = a.shape
        n = b.shape[-2]
    else:
        k, m = a.shape
        n = b.shape[-1]
    row_mult = 16
    tm = _tile(m, pm, row_mult if mode != "tn" else LANE)
    tn = _tile(n, pn)
    tk = _tile(k, pk, LANE if mode != "tn" else row_mult)
    nk = k // tk
    dn = {"nn": (((1,), (0,)), ((), ())), "nt": (((1,), (1,)), ((), ())), "tn": (((0,), (0,)), ((), ()))}[mode]

    def body(*refs):
        if extra is not None:
            a_ref, b_ref, e_ref, o_ref, acc_ref = refs
        else:
            a_ref, b_ref, o_ref, acc_ref = refs
        kk = pl.program_id(2)

        @pl.when(kk == 0)
        def _():
            acc_ref[...] = jnp.zeros_like(acc_ref)

        acc_ref[...] += _dot3(_split(a_ref[...]), _split(b_ref[...]), dn[0])

        @pl.when(kk == nk - 1)
        def _():
            r = acc_ref[...]
            if epilogue == "relu2":
                r = jnp.square(jnp.maximum(r, 0.0))
            elif epilogue == "dsqrelu":
                r = r * (2.0 * jnp.sqrt(e_ref[...].astype(F32)))
            o_ref[...] = r.astype(o_ref.dtype)

    if mode == "tn":
        a_spec = pl.BlockSpec((tk, tm), lambda i, j, kk: (kk, i))
    else:
        a_spec = pl.BlockSpec((tm, tk), lambda i, j, kk: (i, kk))
    if mode == "nt":
        bshape, bidx = (tn, tk), (lambda i, j, kk: (j, kk))
    else:
        bshape, bidx = (tk, tn), (lambda i, j, kk: (kk, j))
    if layer is not None:
        b_spec = pl.BlockSpec((None,) + bshape, lambda i, j, kk: (layer,) + bidx(i, j, kk))
    else:
        b_spec = pl.BlockSpec(bshape, bidx)
    in_specs = [a_spec, b_spec]
    args = [a, b]
    if extra is not None:
        in_specs.append(pl.BlockSpec((tm, tn), lambda i, j, kk: (i, j)))
        args.append(extra)
    return pl.pallas_call(
        body, name=name,
        out_shape=jax.ShapeDtypeStruct((m, n), out_dtype),
        grid=(m // tm, n // tn, nk),
        in_specs=in_specs,
        out_specs=pl.BlockSpec((tm, tn), lambda i, j, kk: (i, j)),
        scratch_shapes=[pltpu.VMEM((tm, tn), F32)],
        compiler_params=_cparams("parallel", "parallel", "arbitrary"),
    )(*args)


def _ln_fwd(a, b, alpha, gamma, beta, tr, name):
    rows, d = a.shape

    def body(*refs):
        if b is not None:
            a_ref, b_ref, g_ref, be_ref, y_ref, yb_ref = refs
            r = alpha * a_ref[...] + b_ref[...]
        else:
            a_ref, g_ref, be_ref, y_ref, yb_ref = refs
            r = a_ref[...]
        mu = jnp.mean(r, axis=-1, keepdims=True)
        xc = r - mu
        var = jnp.mean(xc * xc, axis=-1, keepdims=True)
        y = xc * lax.rsqrt(var + LN_EPS) * g_ref[...] + be_ref[...]
        y_ref[...] = y
        yb_ref[...] = y.astype(yb_ref.dtype)

    row = pl.BlockSpec((tr, d), lambda i: (i, 0))
    vec = pl.BlockSpec((1, d), lambda i: (0, 0))
    args = [a] + ([b] if b is not None else []) + [gamma.reshape(1, d), beta.reshape(1, d)]
    in_specs = [row] + ([row] if b is not None else []) + [vec, vec]
    return pl.pallas_call(
        body, name=name,
        out_shape=(jax.ShapeDtypeStruct((rows, d), F32), jax.ShapeDtypeStruct((rows, d), MM_DTYPE)),
        grid=(rows // tr,), in_specs=in_specs, out_specs=(row, row),
        compiler_params=_cparams("parallel"),
    )(*args)


def _ln_bwd(a, b, alpha, gamma, dy1, dy2, alpha_dy, tr, name, meta_rows=0, tiles_per_seq=1):
    rows, d = a.shape

    def body(*refs):
        refs = list(refs)
        a_ref = refs.pop(0)
        b_ref = refs.pop(0) if b is not None else None
        g_ref = refs.pop(0)
        dy1_ref = refs.pop(0)
        dy2_ref = refs.pop(0) if dy2 is not None else None
        dr_ref, drb_ref, dg_ref, db_ref = refs[:4]
        i = pl.program_id(0)
        r = a_ref[...] if b is None else alpha * a_ref[...] + b_ref[...]
        dy = dy1_ref[...] if dy2 is None else alpha_dy * dy1_ref[...] + dy2_ref[...]
        mu = jnp.mean(r, axis=-1, keepdims=True)
        xc = r - mu
        var = jnp.mean(xc * xc, axis=-1, keepdims=True)
        rstd = lax.rsqrt(var + LN_EPS)
        xhat = xc * rstd
        dxh = dy * g_ref[...]
        m1 = jnp.mean(dxh, axis=-1, keepdims=True)
        m2 = jnp.mean(dxh * xhat, axis=-1, keepdims=True)
        dr = rstd * (dxh - m1 - xhat * m2)
        dr_ref[...] = dr
        drb_ref[...] = dr.astype(drb_ref.dtype)

        @pl.when(i == 0)
        def _():
            dg_ref[...] = jnp.zeros_like(dg_ref)
            db_ref[...] = jnp.zeros_like(db_ref)
            if meta_rows:
                refs[4][...] = jnp.zeros_like(refs[4])

        dg_ref[...] += jnp.sum(dy * xhat, axis=0, keepdims=True)
        db_ref[...] += jnp.sum(dy, axis=0, keepdims=True)
        if meta_rows:
            @pl.when(i % tiles_per_seq == 0)
            def _():
                refs[4][...] += dr[:meta_rows, :]

    row = pl.BlockSpec((tr, d), lambda i: (i, 0))
    vec = pl.BlockSpec((1, d), lambda i: (0, 0))
    args = [a] + ([b] if b is not None else []) + [gamma.reshape(1, d), dy1] + ([dy2] if dy2 is not None else [])
    in_specs = [row] + ([row] if b is not None else []) + [vec, row] + ([row] if dy2 is not None else [])
    out_shape = [jax.ShapeDtypeStruct((rows, d), F32), jax.ShapeDtypeStruct((rows, d), MM_DTYPE),
                 jax.ShapeDtypeStruct((1, d), F32), jax.ShapeDtypeStruct((1, d), F32)]
    out_specs = [row, row, vec, vec]
    if meta_rows:
        out_shape.append(jax.ShapeDtypeStruct((meta_rows, d), F32))
        out_specs.append(pl.BlockSpec((meta_rows, d), lambda i: (0, 0)))
    return pl.pallas_call(
        body, name=name, out_shape=tuple(out_shape), grid=(rows // tr,), in_specs=in_specs,
        out_specs=tuple(out_specs), compiler_params=_cparams("arbitrary"),
    )(*args)


def _loss(y, tgt, seq, tr, tiles_per_seq, name):
    rows, d = y.shape

    def body(y_ref, t_ref, loss_ref, dy_ref):
        i = pl.program_id(0)
        t_in_seq = (i % tiles_per_seq) * tr + lax.broadcasted_iota(jnp.int32, (tr, 1), 0)
        valid = (t_in_seq >= N_META) & (t_in_seq < N_META + seq)
        diff = jnp.where(valid, y_ref[...] - t_ref[...], 0.0)
        dy_ref[...] = diff * (1.0 / d)

        @pl.when(i == 0)
        def _():
            loss_ref[...] = jnp.zeros_like(loss_ref)

        loss_ref[...] += jnp.full(loss_ref.shape, 0.5 / d, F32) * jnp.sum(diff * diff)

    row = pl.BlockSpec((tr, d), lambda i: (i, 0))
    return pl.pallas_call(
        body, name=name,
        out_shape=(jax.ShapeDtypeStruct((1, LANE), F32), jax.ShapeDtypeStruct((rows, d), F32)),
        grid=(rows // tr,), in_specs=[row, row],
        out_specs=(pl.BlockSpec((1, LANE), lambda i: (0, 0)), row),
        compiler_params=_cparams("arbitrary"),
    )(y, tgt)


def _dot(a, b, dims):
    return lax.dot_general(a, b, (dims, ((), ())), preferred_element_type=F32)


def _split(x):
    hi = x.astype(SPLIT_DTYPE)
    lo = (x - hi.astype(F32)).astype(SPLIT_DTYPE)
    return hi, lo


def _dot3(ap, bp, dims):
    return _dot(ap[0], bp[0], dims) + (_dot(ap[0], bp[1], dims) + _dot(ap[1], bp[0], dims))


def _tri_sums(xs, tri):
    parts = []
    for x in xs:
        parts.extend(_split(x))
    out = _dot(jnp.concatenate(parts, axis=0), tri, ((1,), (0,)))
    n = xs[0].shape[0]
    return [out[2 * i * n:(2 * i + 1) * n] + out[(2 * i + 1) * n:(2 * i + 2) * n] for i in range(len(xs))]


def _split_wide(x):
    hi = x.astype(SPLIT_DTYPE).astype(F32)
    return hi, x - hi


def _lhs3(x, axis):
    hi, lo = _split_wide(x)
    return jnp.concatenate([hi, hi, lo], axis=axis).astype(SPLIT_DTYPE)


def _rhs3(x, axis):
    hi, lo = _split_wide(x)
    return jnp.concatenate([hi, lo, hi], axis=axis).astype(SPLIT_DTYPE)


def _tri_masks():
    row = lax.broadcasted_iota(jnp.int32, (QBLK, QBLK), 0)
    col = lax.broadcasted_iota(jnp.int32, (QBLK, QBLK), 1)
    return row, col


def _attn_fwd(u, n_seq, t_rows, cw, name):
    rows = u.shape[0]
    nb = t_rows // QBLK
    width = min(2 * cw, ATTN_LANES)
    n_hd = width // HEAD_DIM
    groups = (2 * cw) // width
    kcol, vcol = (2 * cw) // width, (4 * cw) // width

    def body(q_ref, k_ref, v_ref, o_ref):
        i = pl.program_id(2)
        row, col = _tri_masks()
        incl = (row >= col).astype(SPLIT_DTYPE)
        past = col < row
        sls = [slice(hd * HEAD_DIM, (hd + 1) * HEAD_DIM) for hd in range(n_hd)]
        q3s = [_lhs3(q_ref[:, sl] * (HEAD_DIM ** -0.5), 1) for sl in sls]

        def tile(j, carries, accs, masked):
            r0 = pl.multiple_of(j * QBLK, QBLK)
            zs = [_dot(q3s[hd], _rhs3(k_ref[pl.ds(r0, QBLK), sl], 1), ((1,), (1,))) for hd, sl in enumerate(sls)]
            lks = []
            for z in zs:
                lk = -(jnp.maximum(z, 0.0) + jnp.log1p(jnp.exp(-jnp.abs(z))))
                lks.append(jnp.where(past, lk, 0.0) if masked else lk)
            cls = _tri_sums(lks, incl)
            new_c, new_a = [], []
            for hd, sl in enumerate(sls):
                a = jnp.exp(zs[hd] + cls[hd] + carries[hd])
                if masked:
                    a = jnp.where(past, a, 0.0)
                v3 = _rhs3(v_ref[pl.ds(r0, QBLK), sl], 0)
                new_a.append(accs[hd] + _dot(_lhs3(a, 1), v3, ((1,), (0,))))
                new_c.append(carries[hd] + cls[hd][:, 0:1])
            return tuple(new_c), tuple(new_a)

        c0 = tuple(jnp.zeros((QBLK, 1), F32) for _ in sls)
        a0 = tuple(jnp.zeros((QBLK, HEAD_DIM), F32) for _ in sls)
        st = tile(i, c0, a0, True)
        st = lax.fori_loop(0, i, lambda jj, c: tile(i - 1 - jj, c[0], c[1], False), st)
        o_ref[...] = jnp.concatenate(st[1], axis=1)

    q_spec = pl.BlockSpec((QBLK, width), lambda s, p, i: (s * nb + i, p))
    k_spec = pl.BlockSpec((t_rows, width), lambda s, p, i: (s, kcol + p))
    v_spec = pl.BlockSpec((t_rows, width), lambda s, p, i: (s, vcol + p))
    return pl.pallas_call(
        body, name=name,
        out_shape=jax.ShapeDtypeStruct((rows, 2 * cw), F32),
        grid=(n_seq, groups, nb), in_specs=[q_spec, k_spec, v_spec],
        out_specs=pl.BlockSpec((QBLK, width), lambda s, p, i: (s * nb + i, p)),
        compiler_params=_cparams("parallel", "parallel", "arbitrary"),
    )(u, u, u)


def _attn_bwd(u, o, do, n_seq, t_rows, cw, name):
    rows = u.shape[0]
    nb = t_rows // QBLK
    width = min(2 * cw, ATTN_LANES)
    n_hd = width // HEAD_DIM
    groups = (2 * cw) // width
    kcol, vcol = (2 * cw) // width, (4 * cw) // width

    def body(q_ref, k_ref, v_ref, o_ref, do_ref, dq_ref, dk_ref, dv_ref, dk_acc, dv_acc):
        i = pl.program_id(2)
        row, col = _tri_masks()
        incl = (row >= col).astype(SPLIT_DTYPE)
        strict = (row > col).astype(SPLIT_DTYPE)
        past = col < row

        @pl.when(i == 0)
        def _():
            dk_acc[...] = jnp.zeros_like(dk_acc)
            dv_acc[...] = jnp.zeros_like(dv_acc)

        sls = [slice(hd * HEAD_DIM, (hd + 1) * HEAD_DIM) for hd in range(n_hd)]
        qs = [q_ref[:, sl] * (HEAD_DIM ** -0.5) for sl in sls]
        q3s = [_lhs3(q, 1) for q in qs]
        q3r = [_rhs3(q, 0) for q in qs]
        do3s = [_lhs3(do_ref[:, sl], 1) for sl in sls]
        do3r = [_rhs3(do_ref[:, sl], 0) for sl in sls]
        dtots = [jnp.sum(do_ref[:, sl] * o_ref[:, sl], axis=-1, keepdims=True) for sl in sls]

        def tile(j, ccs, cgs, dqs, masked):
            r0 = pl.multiple_of(j * QBLK, QBLK)
            ks = [k_ref[pl.ds(r0, QBLK), sl] for sl in sls]
            vs = [v_ref[pl.ds(r0, QBLK), sl] for sl in sls]
            zs = [_dot(q3s[hd], _rhs3(ks[hd], 1), ((1,), (1,))) for hd in range(n_hd)]
            das = [_dot(do3s[hd], _rhs3(vs[hd], 1), ((1,), (1,))) for hd in range(n_hd)]
            lks, sigs = [], []
            for z in zs:
                e = jnp.exp(-jnp.abs(z))
                lk = -(jnp.maximum(z, 0.0) + jnp.log1p(e))
                sigs.append(jnp.where(z >= 0.0, 1.0, e) / (1.0 + e))
                lks.append(jnp.where(past, lk, 0.0) if masked else lk)
            cls = _tri_sums(lks, incl)
            a_s, gs = [], []
            for hd in range(n_hd):
                a = jnp.exp(zs[hd] + cls[hd] + ccs[hd])
                if masked:
                    a = jnp.where(past, a, 0.0)
                a_s.append(a)
                gs.append(a * das[hd])
            gxs = _tri_sums(gs, strict)
            n_cc, n_cg, n_dq = [], [], []
            for hd in range(n_hd):
                dz = gs[hd] - sigs[hd] * (dtots[hd] - cgs[hd] - gxs[hd])
                if masked:
                    dz = jnp.where(past, dz, 0.0)
                n_dq.append(dqs[hd] + _dot(_lhs3(dz, 1), _rhs3(ks[hd], 0), ((1,), (0,))))
                dk_acc[hd, pl.ds(r0, QBLK), :] += _dot(_lhs3(dz, 0), q3r[hd], ((0,), (0,)))
                dv_acc[hd, pl.ds(r0, QBLK), :] += _dot(_lhs3(a_s[hd], 0), do3r[hd], ((0,), (0,)))
                n_cc.append(ccs[hd] + cls[hd][:, 0:1])
                n_cg.append(cgs[hd] + gxs[hd][:, 0:1] + gs[hd][:, 0:1])
            return tuple(n_cc), tuple(n_cg), tuple(n_dq)

        z1 = tuple(jnp.zeros((QBLK, 1), F32) for _ in sls)
        dq0 = tuple(jnp.zeros((QBLK, HEAD_DIM), F32) for _ in sls)
        st = tile(i, z1, z1, dq0, True)
        st = lax.fori_loop(0, i, lambda jj, c: tile(i - 1 - jj, c[0], c[1], c[2], False), st)
        dq_ref[...] = (jnp.concatenate(st[2], axis=1) * (HEAD_DIM ** -0.5)).astype(dq_ref.dtype)

        @pl.when(i == nb - 1)
        def _():
            dk_ref[...] = jnp.concatenate([dk_acc[h] for h in range(n_hd)], axis=1).astype(dk_ref.dtype)
            dv_ref[...] = jnp.concatenate([dv_acc[h] for h in range(n_hd)], axis=1).astype(dv_ref.dtype)

    blk = pl.BlockSpec((QBLK, width), lambda s, p, i: (s * nb + i, p))
    k_spec = pl.BlockSpec((t_rows, width), lambda s, p, i: (s, kcol + p))
    v_spec = pl.BlockSpec((t_rows, width), lambda s, p, i: (s, vcol + p))
    seq_blk = pl.BlockSpec((t_rows, width), lambda s, p, i: (s, p))
    shp = jax.ShapeDtypeStruct((rows, 2 * cw), MM_DTYPE)
    return pl.pallas_call(
        body, name=name, out_shape=(shp, shp, shp),
        grid=(n_seq, groups, nb), in_specs=[blk, k_spec, v_spec, blk, blk],
        out_specs=(blk, seq_blk, seq_blk),
        scratch_shapes=[pltpu.VMEM((n_hd, t_rows, HEAD_DIM), F32), pltpu.VMEM((n_hd, t_rows, HEAD_DIM), F32)],
        compiler_params=_cparams("parallel", "parallel", "arbitrary"),
    )(u, u, u, o, do)


def _shifted(xw, shift):
    n = xw.shape[0]
    s = shift % n
    return xw if s == 0 else pltpu.roll(xw, s, 0)


def _causal_conv(xw, w_ref, kw, shift_sign=1):
    acc = None
    for kk in range(kw):
        term = w_ref[kk:kk + 1, :] * _shifted(xw, shift_sign * (kw - 1 - kk))
        acc = term if acc is None else acc + term
    return acc


def _rms(o, g):
    r = lax.rsqrt(jnp.mean(o * o, axis=-1, keepdims=True) + RMS_EPS)
    return o * r * g, r


def _rms_bwd(o, g, dy):
    r = lax.rsqrt(jnp.mean(o * o, axis=-1, keepdims=True) + RMS_EPS)
    gdy = g * dy
    do = gdy * r - o * (r * r * r) * jnp.mean(gdy * o, axis=-1, keepdims=True)
    return do, dy * o * r


def _mix_specs(rows, cw, halo_prev, halo_next):
    tr = CONV_ROWS
    per = tr // HALO
    last = rows // HALO - 1

    def group(cb):
        specs = []
        if halo_prev:
            specs.append(pl.BlockSpec((HALO, cw), lambda i: (jnp.maximum(i * per - 1, 0), cb)))
        specs.append(pl.BlockSpec((tr, cw), lambda i: (i, cb)))
        if halo_next:
            specs.append(pl.BlockSpec((HALO, cw), lambda i: (jnp.minimum((i + 1) * per, last), cb)))
        return specs
    return group


def _window(prev_ref, cur_ref, first):
    return jnp.concatenate([jnp.where(first, 0.0, prev_ref[...]), cur_ref[...]], axis=0)


def _mix_fwd(u, o_sb, w31, b31, lng, lnb, w3, gmix, t_rows, cw, name):
    rows = u.shape[0]
    d = 4 * cw
    tr = CONV_ROWS
    tiles_per_seq = t_rows // tr
    k31, k3 = w31.shape[0], w3.shape[0]
    group = _mix_specs(rows, cw, True, False)

    def body(ap, ac, gp, gc, bc, cp, cc, hp, hc, osb_ref, w31_ref, b31_ref, lng_ref, lnb_ref, w3_ref, gm_ref, y_ref):
        first = (pl.program_id(0) % tiles_per_seq) == 0
        aw = _window(ap, ac, first)
        gw = _window(gp, gc, first)
        hcw = aw * jax.nn.sigmoid(gw)
        yc = (_causal_conv(hcw, w31_ref, k31) + b31_ref[...])[HALO:]
        mu = jnp.mean(yc, axis=-1, keepdims=True)
        xc = yc - mu
        var = jnp.mean(xc * xc, axis=-1, keepdims=True)
        lc = xc * lax.rsqrt(var + LN_EPS) * lng_ref[...] + lnb_ref[...]
        oc = lc * jax.nn.sigmoid(lc)
        pw = _window(cp, cc, first) * _window(hp, hc, first)
        osc = bc[...] * _causal_conv(pw, w3_ref, k3)[HALO:]
        gm = gm_ref[...]
        y_sb, _ = _rms(osb_ref[...], gm[:, :2 * cw])
        y_c, _ = _rms(oc, gm[:, 2 * cw:3 * cw])
        y_s, _ = _rms(osc, gm[:, 3 * cw:])
        y_ref[...] = jnp.concatenate([y_sb, y_c, y_s], axis=1).astype(y_ref.dtype)

    full = lambda arr: pl.BlockSpec(arr.shape, lambda i: (0, 0))
    small = [w31, b31.reshape(1, cw), lng.reshape(1, cw), lnb.reshape(1, cw), w3, gmix.reshape(1, d)]
    in_specs = (group(6) + group(7) + [pl.BlockSpec((tr, cw), lambda i: (i, 8))] + group(9) + group(10)
                + [pl.BlockSpec((tr, 2 * cw), lambda i: (i, 0))] + [full(s) for s in small])
    return pl.pallas_call(
        body, name=name, out_shape=jax.ShapeDtypeStruct((rows, d), MM_DTYPE),
        grid=(rows // tr,), in_specs=in_specs, out_specs=pl.BlockSpec((tr, d), lambda i: (i, 0)),
        compiler_params=_cparams("parallel"),
    )(u, u, u, u, u, u, u, u, u, o_sb, *small)


def _mix_bwd(u, o_sb, dy, w31, b31, lng, lnb, w3, gmix, t_rows, cw, name):
    rows = u.shape[0]
    d = 4 * cw
    tr = CONV_ROWS
    tiles_per_seq = t_rows // tr
    k31, k3 = w31.shape[0], w3.shape[0]
    group = _mix_specs(rows, cw, True, True)
    cur = slice(HALO, HALO + tr)

    def body(*refs):
        refs = list(refs)
        take = lambda n: [refs.pop(0) for _ in range(n)]
        a3, g3, b3, c3, h3, dyc3, dys3 = (take(3) for _ in range(7))
        osb_ref, dysb_ref, w31_ref, b31_ref, lng_ref, lnb_ref, w3_ref, gm_ref = take(8)
        dosb_ref, du_ref, dgm_ref, dw31_ref, db31_ref, dlng_ref, dlnb_ref, dw3_ref = refs
        i = pl.program_id(0)
        first = (i % tiles_per_seq) == 0
        last = (i % tiles_per_seq) == tiles_per_seq - 1
        wrow = lax.broadcasted_iota(jnp.int32, (tr + 2 * HALO, 1), 0)
        keep = jnp.logical_not((first & (wrow < HALO)) | (last & (wrow >= HALO + tr)))

        def win(r3):
            return jnp.concatenate([r3[0][...], r3[1][...], r3[2][...]], axis=0)

        gm = gm_ref[...]
        aw, gw, bw, cwin, hw = win(a3), win(g3), win(b3), win(c3), win(h3)
        dy_c = jnp.where(keep, win(dyc3), 0.0)
        dy_s = jnp.where(keep, win(dys3), 0.0)
        sg = jax.nn.sigmoid(gw)
        hcw = jnp.where(keep, aw * sg, 0.0)
        yc = _causal_conv(hcw, w31_ref, k31) + b31_ref[...]
        mu = jnp.mean(yc, axis=-1, keepdims=True)
        xc = yc - mu
        rstd = lax.rsqrt(jnp.mean(xc * xc, axis=-1, keepdims=True) + LN_EPS)
        nrm = xc * rstd
        lc = nrm * lng_ref[...] + lnb_ref[...]
        sl = jax.nn.sigmoid(lc)
        oc = lc * sl
        d_oc, dg_c = _rms_bwd(oc, gm[:, 2 * cw:3 * cw], dy_c)
        dlc = d_oc * sl * (1.0 + lc * (1.0 - sl))
        dn = dlc * lng_ref[...]
        dyc = rstd * (dn - jnp.mean(dn, axis=-1, keepdims=True) - nrm * jnp.mean(dn * nrm, axis=-1, keepdims=True))
        dyc = jnp.where(keep, dyc, 0.0)
        dhc = _causal_conv(dyc, w31_ref, k31, -1)
        da = dhc * sg
        dgate = dhc * aw * sg * (1.0 - sg)
        pw = jnp.where(keep, cwin * hw, 0.0)
        cs = _causal_conv(pw, w3_ref, k3)
        osc = bw * cs
        d_osc, dg_s = _rms_bwd(osc, gm[:, 3 * cw:], dy_s)
        dcs = jnp.where(keep, d_osc * bw, 0.0)
        dbg = d_osc * cs
        dp = _causal_conv(dcs, w3_ref, k3, -1)
        dcg = dp * hw
        dhs = dp * cwin
        du_ref[...] = jnp.concatenate([x[cur] for x in (da, dgate, dbg, dcg, dhs)], axis=1).astype(du_ref.dtype)
        d_osb, dg_sb = _rms_bwd(osb_ref[...], gm[:, :2 * cw], dysb_ref[...])
        dosb_ref[...] = d_osb

        @pl.when(i == 0)
        def _():
            for r in (dgm_ref, dw31_ref, db31_ref, dlng_ref, dlnb_ref, dw3_ref):
                r[...] = jnp.zeros_like(r)

        csum = lambda x: jnp.sum(x, axis=0, keepdims=True)
        dgm_ref[...] += jnp.concatenate([csum(dg_sb), csum(dg_c[cur]), csum(dg_s[cur])], axis=1)
        db31_ref[...] += csum(dyc[cur])
        dlng_ref[...] += csum((dlc * nrm)[cur])
        dlnb_ref[...] += csum(dlc[cur])
        for kk in range(k31):
            dw31_ref[kk:kk + 1, :] += csum((dyc * _shifted(hcw, k31 - 1 - kk))[cur])
        for kk in range(k3):
            dw3_ref[kk:kk + 1, :] += csum((dcs * _shifted(pw, k3 - 1 - kk))[cur])

    full = lambda arr: pl.BlockSpec(arr.shape, lambda i: (0, 0))
    small = [w31, b31.reshape(1, cw), lng.reshape(1, cw), lnb.reshape(1, cw), w3, gmix.reshape(1, d)]
    in_specs = (group(6) + group(7) + group(8) + group(9) + group(10) + group(2) + group(3)
                + [pl.BlockSpec((tr, 2 * cw), lambda i: (i, 0))] * 2 + [full(s) for s in small])
    acc = lambda r, c: (jax.ShapeDtypeStruct((r, c), F32), pl.BlockSpec((r, c), lambda i: (0, 0)))
    outs = [(jax.ShapeDtypeStruct((rows, 2 * cw), F32), pl.BlockSpec((tr, 2 * cw), lambda i: (i, 0))),
            (jax.ShapeDtypeStruct((rows, 5 * cw), MM_DTYPE), pl.BlockSpec((tr, 5 * cw), lambda i: (i, 0))),
            acc(1, d), acc(k31, cw), acc(1, cw), acc(1, cw), acc(1, cw), acc(k3, cw)]
    return pl.pallas_call(
        body, name=name, out_shape=tuple(o[0] for o in outs), grid=(rows // tr,), in_specs=in_specs,
        out_specs=tuple(o[1] for o in outs), compiler_params=_cparams("arbitrary"),
    )(*([u] * 15), *([dy] * 6), o_sb, dy, *small)


def _position():
    return lax.axis_index("x"), lax.axis_index("y"), lax.axis_index("c")


def _all_gather(xs, name):
    n_w = len(xs)
    per = N_DEV - 1

    def body(*refs):
        x_refs, out_refs = refs[:n_w], refs[n_w:2 * n_w]
        send_sems, recv_sems, local_sems = refs[2 * n_w:]
        mx, my, mc = _position()
        me, sibling = (mx, my, mc), (mx, my, 1 - mc)
        chips = [(1 - mx, my), (mx, 1 - my), (1 - mx, 1 - my)]
        ws = range(n_w)

        def slot(w, px, py, pc):
            return out_refs[w].at[4 * px + 2 * py + pc]

        def copy(w, k, block, to, own=False):
            return pltpu.make_async_remote_copy(
                src_ref=x_refs[w] if own else slot(w, *block), dst_ref=slot(w, *block),
                send_sem=send_sems.at[per * w + k], recv_sem=recv_sems.at[per * w + k],
                device_id=to, device_id_type=MESH_IDS)

        mine = [pltpu.make_async_copy(x_refs[w], slot(w, *me), local_sems.at[w]) for w in ws]
        for cp in mine:
            cp.start()
        first = [copy(w, 1 + j, me, (*chip, mc), own=True) for w in ws for j, chip in enumerate(chips)]
        first += [copy(w, 0, me, sibling, own=True) for w in ws]
        for cp in first:
            cp.start()
        passed = []
        for j, chip in enumerate(chips):
            for w in ws:
                copy(w, 1 + j, (*chip, mc), me).wait_recv()
                fwd = copy(w, 4 + j, (*chip, mc), sibling)
                fwd.start()
                passed.append(fwd)
        for w in ws:
            copy(w, 0, sibling, me).wait_recv()
        for j, chip in enumerate(chips):
            for w in ws:
                copy(w, 4 + j, (*chip, 1 - mc), me).wait_recv()
        for cp in first + passed:
            cp.wait_send()
        for cp in mine:
            cp.wait()

    hbm = pl.BlockSpec(memory_space=pl.ANY)
    return pl.pallas_call(
        body, name=name, out_shape=tuple(jax.ShapeDtypeStruct((N_DEV,) + x.shape, x.dtype) for x in xs),
        in_specs=[hbm] * n_w, out_specs=tuple([hbm] * n_w),
        scratch_shapes=[pltpu.SemaphoreType.DMA((per * n_w,)), pltpu.SemaphoreType.DMA((per * n_w,)),
                        pltpu.SemaphoreType.DMA((n_w,))],
    )(*xs)


def _all_to_all(sends, name):
    n_w = len(sends)

    def body(*refs):
        send_refs, recv_refs = refs[:n_w], refs[n_w:2 * n_w]
        send_sems, recv_sems = refs[2 * n_w:]
        mx, my, mc = _position()
        copies = []
        for k in range(1, N_DEV):
            px, py, pc = mx ^ (k >> 2), my ^ ((k >> 1) & 1), mc ^ (k & 1)
            for w in range(n_w):
                copies.append(pltpu.make_async_remote_copy(
                    src_ref=send_refs[w].at[4 * px + 2 * py + pc], dst_ref=recv_refs[w].at[k - 1],
                    send_sem=send_sems.at[w * (N_DEV - 1) + k - 1], recv_sem=recv_sems.at[w * (N_DEV - 1) + k - 1],
                    device_id=(px, py, pc), device_id_type=MESH_IDS))
        for cp in copies:
            cp.start()
        for cp in copies:
            cp.wait_recv()
        for cp in copies:
            cp.wait_send()

    hbm = pl.BlockSpec(memory_space=pl.ANY)
    n_sem = n_w * (N_DEV - 1)
    return pl.pallas_call(
        body, name=name,
        out_shape=tuple(jax.ShapeDtypeStruct((N_DEV - 1,) + s.shape[1:], s.dtype) for s in sends),
        in_specs=[hbm] * n_w, out_specs=tuple([hbm] * n_w),
        scratch_shapes=[pltpu.SemaphoreType.DMA((n_sem,)), pltpu.SemaphoreType.DMA((n_sem,))],
    )(*sends)


def _adam_math(w, g, m, v):
    m = ADAM_B1 * m + (1.0 - ADAM_B1) * g
    v = ADAM_B2 * v + (1.0 - ADAM_B2) * (g * g)
    m_hat = m / (1.0 - ADAM_B1 ** ADAM_STEP)
    v_hat = v / (1.0 - ADAM_B2 ** ADAM_STEP)
    delta = -ADAM_LR * (m_hat / (jnp.sqrt(v_hat) + ADAM_EPS) + ADAM_WD * w)
    return delta, m, v


def _adam(g_own, recv, w, m, v, name):
    rows, cols = w.shape
    tr = _tile(rows, max(8, ADAM_BLOCK_ELEMS // cols), 8)

    def body(*refs):
        if recv is not None:
            g_ref, r_ref, w_ref, m_ref, v_ref, go_ref, d_ref, mo_ref, vo_ref = refs
            g = g_ref[...]
            for k in range(recv.shape[0]):
                g = g + r_ref[k].astype(F32)
        else:
            g_ref, w_ref, m_ref, v_ref, go_ref, d_ref, mo_ref, vo_ref = refs
            g = g_ref[...]
        delta, mn, vn = _adam_math(w_ref[...], g, m_ref[...], v_ref[...])
        go_ref[...] = g
        d_ref[...] = delta
        mo_ref[...] = mn
        vo_ref[...] = vn

    blk = pl.BlockSpec((tr, cols), lambda i: (i, 0))
    in_specs = [blk] + ([pl.BlockSpec((recv.shape[0], tr, cols), lambda i: (0, i, 0))] if recv is not None else []) + [blk] * 3
    args = [g_own] + ([recv] if recv is not None else []) + [w, m, v]
    shp = jax.ShapeDtypeStruct((rows, cols), F32)
    return pl.pallas_call(
        body, name=name, out_shape=(shp,) * 4, grid=(rows // tr,), in_specs=in_specs, out_specs=(blk,) * 4,
        compiler_params=_cparams("parallel"),
    )(*args)


def _sum_devices(parts, name):
    n, rows, cols = parts.shape

    def body(p_ref, o_ref):
        acc = p_ref[0]
        for k in range(1, n):
            acc = acc + p_ref[k]
        o_ref[...] = acc

    return pl.pallas_call(
        body, name=name, out_shape=jax.ShapeDtypeStruct((rows, cols), F32),
        in_specs=[pl.BlockSpec(memory_space=pltpu.VMEM)], out_specs=pl.BlockSpec(memory_space=pltpu.VMEM),
    )(parts)


def _pack(arrs, dtype):
    flat = jnp.concatenate([a.reshape(-1).astype(dtype) for a in arrs])
    pad = (-flat.shape[0]) % (16 * LANE)
    return jnp.pad(flat, (0, pad)).reshape(-1, LANE)


def _unpack(flat2d, shapes):
    flat = flat2d.reshape(-1)
    out, off = [], 0
    for s in shapes:
        n = 1
        for dim in s:
            n *= dim
        out.append(flat[off:off + n].reshape(s))
        off += n
    return out


def _unpack_gathered(g, shapes):
    flat = g.reshape(N_DEV, -1)
    out, off = [], 0
    for s in shapes:
        n = 1
        for dim in s:
            n *= dim
        out.append(flat[:, off:off + n].reshape((N_DEV,) + tuple(s)))
        off += n
    return out


def _cols_from_devices(g):
    nd = g.ndim
    perm = tuple(range(1, nd - 1)) + (0, nd - 1)
    t = jnp.transpose(g, perm)
    return t.reshape(t.shape[:-2] + (t.shape[-2] * t.shape[-1],))


def _cols_to_devices(a, dtype):
    c = a.shape[-1] // N_DEV
    t = a.reshape(a.shape[:-1] + (N_DEV, c)).astype(dtype)
    nd = t.ndim
    return jnp.transpose(t, (nd - 2,) + tuple(range(nd - 2)) + (nd - 1,))


def _rows_to_devices(a, dtype):
    nl, r8, c = a.shape
    return jnp.transpose(a.reshape(nl, N_DEV, r8 // N_DEV, c).astype(dtype), (1, 0, 2, 3))


def kernel(x, meta_tokens, ln_in_g, ln_in_b, w_in, w_conf_dw, b_conf_dw, ln_conf_g, ln_conf_b, w_short_dw, g_mix, w_out, ln_mix_g, ln_mix_b, w_ff1, w_ff2, ln_ff_g, ln_ff_b, loss_target, m_meta_tokens, m_ln_in_g, m_ln_in_b, m_w_in, m_w_conf_dw, m_b_conf_dw, m_ln_conf_g, m_ln_conf_b, m_w_short_dw, m_g_mix, m_w_out, m_ln_mix_g, m_ln_mix_b, m_w_ff1, m_w_ff2, m_ln_ff_g, m_ln_ff_b, v_meta_tokens, v_ln_in_g, v_ln_in_b, v_w_in, v_w_conf_dw, v_b_conf_dw, v_ln_conf_g, v_ln_conf_b, v_w_short_dw, v_g_mix, v_w_out, v_ln_mix_g, v_ln_mix_b, v_w_ff1, v_w_ff2, v_ln_ff_g, v_ln_ff_b):
    n_seq, seq, d = x.shape
    depth = w_in.shape[0]
    cw = d // 4
    alpha = (2.0 * depth) ** 0.25
    t_rows = -(-(N_META + seq) // QBLK) * QBLK
    rows = n_seq * t_rows
    tr = _row_tile(t_rows)
    tiles_per_seq = t_rows // tr
    me = 4 * lax.axis_index("x") + 2 * lax.axis_index("y") + lax.axis_index("c")

    small_shards = [meta_tokens, w_conf_dw, w_short_dw]
    sm = _unpack_gathered(_all_gather([_pack(small_shards, F32)], "gather_small_params")[0],
                          [a.shape for a in small_shards])
    meta_full, w31_full, w3_full = (_cols_from_devices(a) for a in sm)
    big_shards = [w_in, w_out, w_ff1, w_ff2]
    bg = _all_gather(big_shards, "gather_weights")
    win_f = _cols_from_devices(bg[0])
    wout_f = jnp.transpose(bg[1], (1, 0, 2, 3)).reshape(depth, d, d)
    wff1_f = _cols_from_devices(bg[2])
    wff2_f = jnp.transpose(bg[3], (1, 0, 2, 3)).reshape(depth, 4 * d, d)

    pad_rows = t_rows - N_META - seq
    xin = jnp.concatenate([jnp.broadcast_to(meta_full[None], (n_seq, N_META, d)), x,
                           jnp.zeros((n_seq, pad_rows, d), F32)], axis=1).reshape(rows, d)
    tgt = jnp.pad(loss_target, ((0, 0), (N_META, pad_rows), (0, 0))).reshape(rows, d)
    h, hb = _ln_fwd(xin, None, 1.0, ln_in_g, ln_in_b, tr, "ln_in")
    saved = []
    for l in range(depth):
        u = _matmul(hb, win_f, "nn", F32, "proj_in", layer=l, pn=1408)
        o_sb = _attn_fwd(u, n_seq, t_rows, cw, "attn_fwd")
        yb = _mix_fwd(u, o_sb, w31_full[l], b_conf_dw[l], ln_conf_g[l], ln_conf_b[l], w3_full[l], g_mix[l],
                      t_rows, cw, "mix_fwd")
        mix = _matmul(yb, wout_f, "nn", F32, "proj_out", layer=l)
        h1, h1b = _ln_fwd(h, mix, alpha, ln_mix_g[l], ln_mix_b[l], tr, "ln_mix")
        act = _matmul(h1b, wff1_f, "nn", MM_DTYPE, "ff1", layer=l, epilogue="relu2")
        ff = _matmul(act, wff2_f, "nn", F32, "ff2", layer=l)
        h2, h2b = _ln_fwd(h1, ff, alpha, ln_ff_g[l], ln_ff_b[l], tr, "ln_ff")
        saved.append((h, hb, u, o_sb, yb, mix, h1, h1b, act, ff))
        h, hb = h2, h2b
    loss_part, dy1 = _loss(h, tgt, seq, tr, tiles_per_seq, "loss")
    loss = lax.psum(loss_part[0, 0], MESH_AXES)

    dy2, a_dy = None, 1.0
    gw_in, gw_out, gw_ff1, gw_ff2 = [], [], [], []
    g_small = {n: [] for n in ("w31", "b31", "lng", "lnb", "w3", "gmix", "lmg", "lmb", "lfg", "lfb")}
    for l in reversed(range(depth)):
        h, hb, u, o_sb, yb, mix, h1, h1b, act, ff = saved[l]
        dr2, dr2b, dg, db = _ln_bwd(h1, ff, alpha, ln_ff_g[l], dy1, dy2, a_dy, tr, "ln_ff_bwd")
        g_small["lfg"].append(dg)
        g_small["lfb"].append(db)
        df1b = _matmul(dr2b, wff2_f, "nt", MM_DTYPE, "ff2_dx", layer=l, epilogue="dsqrelu", extra=act)
        gw_ff2.append(_matmul(act, dr2b, "tn", F32, "ff2_dw", pm=1024, pk=544))
        dh1 = _matmul(df1b, wff1_f, "nt", F32, "ff1_dx", layer=l)
        gw_ff1.append(_matmul(h1b, df1b, "tn", F32, "ff1_dw", pm=1024, pk=544))
        dr1, dr1b, dg, db = _ln_bwd(h, mix, alpha, ln_mix_g[l], dr2, dh1, alpha, tr, "ln_mix_bwd")
        g_small["lmg"].append(dg)
        g_small["lmb"].append(db)
        dyr = _matmul(dr1b, wout_f, "nt", F32, "proj_out_dx", layer=l)
        gw_out.append(_matmul(yb, dr1b, "tn", F32, "proj_out_dw", pm=1024, pk=544))
        d_osb, du_conv, dgm, dw31, db31, dlng, dlnb, dw3 = _mix_bwd(
            u, o_sb, dyr, w31_full[l], b_conf_dw[l], ln_conf_g[l], ln_conf_b[l], w3_full[l], g_mix[l],
            t_rows, cw, "mix_bwd")
        for n, val in zip(("gmix", "w31", "b31", "lng", "lnb", "w3"), (dgm, dw31, db31, dlng, dlnb, dw3)):
            g_small[n].append(val)
        dq, dk, dv = _attn_bwd(u, o_sb, d_osb, n_seq, t_rows, cw, "attn_bwd")
        du = jnp.concatenate([dq, dk, dv, du_conv], axis=1)
        dh = _matmul(du, win_f, "nt", F32, "proj_in_dx", layer=l, pk=1408)
        gw_in.append(_matmul(hb, du, "tn", F32, "proj_in_dw", pm=1024, pn=1408, pk=544))
        dy1, dy2, a_dy = dr1, dh, alpha
    dxin, _, dg_in, db_in, dmeta = _ln_bwd(xin, None, 1.0, ln_in_g, dy1, dy2, a_dy, tr, "ln_in_bwd",
                                           meta_rows=N_META, tiles_per_seq=tiles_per_seq)
    grad_x = dxin.reshape(n_seq, t_rows, d)[:, N_META:N_META + seq]

    stack = lambda lst: jnp.stack(lst[::-1])
    gw_in, gw_out, gw_ff1, gw_ff2 = stack(gw_in), stack(gw_out), stack(gw_ff1), stack(gw_ff2)
    sends = [_cols_to_devices(gw_in, COMM_GRAD_DTYPE), _rows_to_devices(gw_out, COMM_GRAD_DTYPE),
             _cols_to_devices(gw_ff1, COMM_GRAD_DTYPE), _rows_to_devices(gw_ff2, COMM_GRAD_DTYPE)]
    recvs = _all_to_all(sends, "exchange_weight_grads")
    own = [lax.dynamic_slice_in_dim(gw_in, me * w_in.shape[2], w_in.shape[2], 2),
           lax.dynamic_slice_in_dim(gw_out, me * w_out.shape[1], w_out.shape[1], 1),
           lax.dynamic_slice_in_dim(gw_ff1, me * w_ff1.shape[2], w_ff1.shape[2], 2),
           lax.dynamic_slice_in_dim(gw_ff2, me * w_ff2.shape[1], w_ff2.shape[1], 1)]
    big = {}
    for nm, g_own, rv, w, m, v in zip(("w_in", "w_out", "w_ff1", "w_ff2"), own, recvs,
                                      (w_in, w_out, w_ff1, w_ff2), (m_w_in, m_w_out, m_w_ff1, m_w_ff2),
                                      (v_w_in, v_w_out, v_w_ff1, v_w_ff2)):
        c = w.shape[-1]
        res = _adam(g_own.reshape(-1, c), rv.reshape(N_DEV - 1, -1, c), w.reshape(-1, c), m.reshape(-1, c),
                    v.reshape(-1, c), "adam_" + nm)
        big[nm] = [r.reshape(w.shape) for r in res]

    sstack = lambda n: jnp.stack(g_small[n][::-1])
    partial = {
        "meta_tokens": dmeta, "ln_in_g": dg_in.reshape(d), "ln_in_b": db_in.reshape(d),
        "w_conf_dw": sstack("w31"), "b_conf_dw": sstack("b31").reshape(depth, cw),
        "ln_conf_g": sstack("lng").reshape(depth, cw), "ln_conf_b": sstack("lnb").reshape(depth, cw),
        "w_short_dw": sstack("w3"), "g_mix": sstack("gmix").reshape(depth, d),
        "ln_mix_g": sstack("lmg").reshape(depth, d), "ln_mix_b": sstack("lmb").reshape(depth, d),
        "ln_ff_g": sstack("lfg").reshape(depth, d), "ln_ff_b": sstack("lfb").reshape(depth, d),
    }
    names = list(partial)
    gathered = _all_gather([_pack([partial[n] for n in names], F32)], "gather_small_grads")[0]
    summed = _unpack(_sum_devices(gathered, "sum_small_grads"), [partial[n].shape for n in names])
    g_full = dict(zip(names, summed))
    local = {"meta_tokens": (meta_tokens, m_meta_tokens, v_meta_tokens), "ln_in_g": (ln_in_g, m_ln_in_g, v_ln_in_g),
             "ln_in_b": (ln_in_b, m_ln_in_b, v_ln_in_b), "w_conf_dw": (w_conf_dw, m_w_conf_dw, v_w_conf_dw),
             "b_conf_dw": (b_conf_dw, m_b_conf_dw, v_b_conf_dw), "ln_conf_g": (ln_conf_g, m_ln_conf_g, v_ln_conf_g),
             "ln_conf_b": (ln_conf_b, m_ln_conf_b, v_ln_conf_b), "w_short_dw": (w_short_dw, m_w_short_dw, v_w_short_dw),
             "g_mix": (g_mix, m_g_mix, v_g_mix), "ln_mix_g": (ln_mix_g, m_ln_mix_g, v_ln_mix_g),
             "ln_mix_b": (ln_mix_b, m_ln_mix_b, v_ln_mix_b), "ln_ff_g": (ln_ff_g, m_ln_ff_g, v_ln_ff_g),
             "ln_ff_b": (ln_ff_b, m_ln_ff_b, v_ln_ff_b)}
    g_loc = {}
    for n in names:
        wloc = local[n][0]
        g = g_full[n]
        if g.shape != wloc.shape:
            g = lax.dynamic_slice_in_dim(g, me * wloc.shape[-1], wloc.shape[-1], g.ndim - 1)
        g_loc[n] = g
    shapes = [g_loc[n].shape for n in names]
    _, sd, smo, svo = _adam(_pack([g_loc[n] for n in names], F32), None, _pack([local[n][0] for n in names], F32),
                            _pack([local[n][1] for n in names], F32), _pack([local[n][2] for n in names], F32),
                            "adam_small")
    small_d, small_m, small_v = (dict(zip(names, _unpack(a, shapes))) for a in (sd, smo, svo))

    order = ["meta_tokens", "ln_in_g", "ln_in_b", "w_in", "w_conf_dw", "b_conf_dw", "ln_conf_g", "ln_conf_b",
             "w_short_dw", "g_mix", "w_out", "ln_mix_g", "ln_mix_b", "w_ff1", "w_ff2", "ln_ff_g", "ln_ff_b"]
    grads = [big[n][0] if n in big else g_loc[n] for n in order]
    deltas = [big[n][1] if n in big else small_d[n] for n in order]
    new_m = [big[n][2] if n in big else small_m[n] for n in order]
    new_v = [big[n][3] if n in big else small_v[n] for n in order]
    return (loss, grad_x, *grads, *deltas, *new_m, *new_v)
```

```python
import functools

import jax
import jax.numpy as jnp
from jax import lax
from jax.experimental import pallas as pl
from jax.experimental.pallas import tpu as pltpu

F32 = jnp.float32
MM_DTYPE = jnp.float32
SPLIT_DTYPE = jnp.bfloat16
COMM_GRAD_DTYPE = jnp.bfloat16
N_DEV = 8
N_META = 16
HEAD_DIM = 64
LANE = 128
QBLK = 128
ATTN_LANES = 256
ATTN_FWD_LANES = 512
HALO = 32
CONV_ROWS = 128
ADAM_BLOCK_ELEMS = 128 * 1024
LN_EPS = 1e-5
RMS_EPS = 1e-6
ADAM_LR = 0.001
ADAM_B1 = 0.9
ADAM_B2 = 0.999
ADAM_EPS = 1e-08
ADAM_WD = 0.01
ADAM_STEP = 10
VMEM_LIMIT_BYTES = 48 * 1024 * 1024
MESH_AXES = ("x", "y", "c")
MESH_IDS = pl.DeviceIdType.MESH


def _cparams(*sem):
    return pltpu.CompilerParams(dimension_semantics=tuple(sem), vmem_limit_bytes=VMEM_LIMIT_BYTES)


def _tile(n, pref, mult=LANE):
    best = None
    t = mult
    while t <= min(n, pref):
        if n % t == 0:
            best = t
        t += mult
    return best if best is not None else n


def _row_tile(t_rows):
    for cand in (272, 256, 128):
        if t_rows % cand == 0:
            return cand
    raise ValueError(t_rows)


def _matmul(a, b, mode, out_dtype, name, layer=None, epilogue=None, extra=None, pm=544, pn=1024, pk=1024):
    if mode == "nn":
        m, k = a.shape
        n = b.shape[-1]
    elif mode == "nt":
        m, k = a.shape
        n = b.shape[-2]
    else:
        k, m = a.shape
        n = b.shape[-1]
    row_mult = 16
    tm = _tile(m, pm, row_mult if mode != "tn" else LANE)
    tn = _tile(n, pn)
    tk = _tile(k, pk, LANE if mode != "tn" else row_mult)
    nk = k // tk
    dn = {"nn": (((1,), (0,)), ((), ())), "nt": (((1,), (1,)), ((), ())), "tn": (((0,), (0,)), ((), ()))}[mode]

    def body(*refs):
        if extra is not None:
            a_ref, b_ref, e_ref, o_ref, acc_ref = refs
        else:
            a_ref, b_ref, o_ref, acc_ref = refs
        kk = pl.program_id(2)

        @pl.when(kk == 0)
        def _():
            acc_ref[...] = jnp.zeros_like(acc_ref)

        acc_ref[...] += _dot3(_split(a_ref[...]), _split(b_ref[...]), dn[0])

        @pl.when(kk == nk - 1)
        def _():
            r = acc_ref[...]
            if epilogue == "relu2":
                r = jnp.square(jnp.maximum(r, 0.0))
            elif epilogue == "dsqrelu":
                r = r * (2.0 * jnp.sqrt(e_ref[...].astype(F32)))
            o_ref[...] = r.astype(o_ref.dtype)

    if mode == "tn":
        a_spec = pl.BlockSpec((tk, tm), lambda i, j, kk: (kk, i))
    else:
        a_spec = pl.BlockSpec((tm, tk), lambda i, j, kk: (i, kk))
    if mode == "nt":
        bshape, bidx = (tn, tk), (lambda i, j, kk: (j, kk))
    else:
        bshape, bidx = (tk, tn), (lambda i, j, kk: (kk, j))
    if layer is not None:
        b_spec = pl.BlockSpec((None,) + bshape, lambda i, j, kk: (layer,) + bidx(i, j, kk))
    else:
        b_spec = pl.BlockSpec(bshape, bidx)
    in_specs = [a_spec, b_spec]
    args = [a, b]
    if extra is not None:
        in_specs.append(pl.BlockSpec((tm, tn), lambda i, j, kk: (i, j)))
        args.append(extra)
    return pl.pallas_call(
        body, name=name,
        out_shape=jax.ShapeDtypeStruct((m, n), out_dtype),
        grid=(m // tm, n // tn, nk),
        in_specs=in_specs,
        out_specs=pl.BlockSpec((tm, tn), lambda i, j, kk: (i, j)),
        scratch_shapes=[pltpu.VMEM((tm, tn), F32)],
        compiler_params=_cparams("parallel", "parallel", "arbitrary"),
    )(*args)


def _ln_fwd(a, b, alpha, gamma, beta, tr, name):
    rows, d = a.shape

    def body(*refs):
        if b is not None:
            a_ref, b_ref, g_ref, be_ref, y_ref, yb_ref = refs
            r = alpha * a_ref[...] + b_ref[...]
        else:
            a_ref, g_ref, be_ref, y_ref, yb_ref = refs
            r = a_ref[...]
        mu = jnp.mean(r, axis=-1, keepdims=True)
        xc = r - mu
        var = jnp.mean(xc * xc, axis=-1, keepdims=True)
        y = xc * lax.rsqrt(var + LN_EPS) * g_ref[...] + be_ref[...]
        y_ref[...] = y
        yb_ref[...] = y.astype(yb_ref.dtype)

    row = pl.BlockSpec((tr, d), lambda i: (i, 0))
    vec = pl.BlockSpec((1, d), lambda i: (0, 0))
    args = [a] + ([b] if b is not None else []) + [gamma.reshape(1, d), beta.reshape(1, d)]
    in_specs = [row] + ([row] if b is not None else []) + [vec, vec]
    return pl.pallas_call(
        body, name=name,
        out_shape=(jax.ShapeDtypeStruct((rows, d), F32), jax.ShapeDtypeStruct((rows, d), MM_DTYPE)),
        grid=(rows // tr,), in_specs=in_specs, out_specs=(row, row),
        compiler_params=_cparams("parallel"),
    )(*args)


def _ln_bwd(a, b, alpha, gamma, dy1, dy2, alpha_dy, tr, name, meta_rows=0, tiles_per_seq=1):
    rows, d = a.shape

    def body(*refs):
        refs = list(refs)
        a_ref = refs.pop(0)
        b_ref = refs.pop(0) if b is not None else None
        g_ref = refs.pop(0)
        dy1_ref = refs.pop(0)
        dy2_ref = refs.pop(0) if dy2 is not None else None
        dr_ref, drb_ref, dg_ref, db_ref = refs[:4]
        i = pl.program_id(0)
        r = a_ref[...] if b is None else alpha * a_ref[...] + b_ref[...]
        dy = dy1_ref[...] if dy2 is None else alpha_dy * dy1_ref[...] + dy2_ref[...]
        mu = jnp.mean(r, axis=-1, keepdims=True)
        xc = r - mu
        var = jnp.mean(xc * xc, axis=-1, keepdims=True)
        rstd = lax.rsqrt(var + LN_EPS)
        xhat = xc * rstd
        dxh = dy * g_ref[...]
        m1 = jnp.mean(dxh, axis=-1, keepdims=True)
        m2 = jnp.mean(dxh * xhat, axis=-1, keepdims=True)
        dr = rstd * (dxh - m1 - xhat * m2)
        dr_ref[...] = dr
        drb_ref[...] = dr.astype(drb_ref.dtype)

        @pl.when(i == 0)
        def _():
            dg_ref[...] = jnp.zeros_like(dg_ref)
            db_ref[...] = jnp.zeros_like(db_ref)
            if meta_rows:
                refs[4][...] = jnp.zeros_like(refs[4])

        dg_ref[...] += jnp.sum(dy * xhat, axis=0, keepdims=True)
        db_ref[...] += jnp.sum(dy, axis=0, keepdims=True)
        if meta_rows:
            @pl.when(i % tiles_per_seq == 0)
            def _():
                refs[4][...] += dr[:meta_rows, :]

    row = pl.BlockSpec((tr, d), lambda i: (i, 0))
    vec = pl.BlockSpec((1, d), lambda i: (0, 0))
    args = [a] + ([b] if b is not None else []) + [gamma.reshape(1, d), dy1] + ([dy2] if dy2 is not None else [])
    in_specs = [row] + ([row] if b is not None else []) + [vec, row] + ([row] if dy2 is not None else [])
    out_shape = [jax.ShapeDtypeStruct((rows, d), F32), jax.ShapeDtypeStruct((rows, d), MM_DTYPE),
                 jax.ShapeDtypeStruct((1, d), F32), jax.ShapeDtypeStruct((1, d), F32)]
    out_specs = [row, row, vec, vec]
    if meta_rows:
        out_shape.append(jax.ShapeDtypeStruct((meta_rows, d), F32))
        out_specs.append(pl.BlockSpec((meta_rows, d), lambda i: (0, 0)))
    return pl.pallas_call(
        body, name=name, out_shape=tuple(out_shape), grid=(rows // tr,), in_specs=in_specs,
        out_specs=tuple(out_specs), compiler_params=_cparams("arbitrary"),
    )(*args)


def _loss(y, tgt, seq, tr, tiles_per_seq, name):
    rows, d = y.shape

    def body(y_ref, t_ref, loss_ref, dy_ref):
        i = pl.program_id(0)
        t_in_seq = (i % tiles_per_seq) * tr + lax.broadcasted_iota(jnp.int32, (tr, 1), 0)
        valid = (t_in_seq >= N_META) & (t_in_seq < N_META + seq)
        diff = jnp.where(valid, y_ref[...] - t_ref[...], 0.0)
        dy_ref[...] = diff * (1.0 / d)

        @pl.when(i == 0)
        def _():
            loss_ref[...] = jnp.zeros_like(loss_ref)

        loss_ref[...] += jnp.full(loss_ref.shape, 0.5 / d, F32) * jnp.sum(diff * diff)

    row = pl.BlockSpec((tr, d), lambda i: (i, 0))
    return pl.pallas_call(
        body, name=name,
        out_shape=(jax.ShapeDtypeStruct((1, LANE), F32), jax.ShapeDtypeStruct((rows, d), F32)),
        grid=(rows // tr,), in_specs=[row, row],
        out_specs=(pl.BlockSpec((1, LANE), lambda i: (0, 0)), row),
        compiler_params=_cparams("arbitrary"),
    )(y, tgt)


def _dot(a, b, dims):
    return lax.dot_general(a, b, (dims, ((), ())), preferred_element_type=F32)


def _split(x):
    hi = x.astype(SPLIT_DTYPE)
    lo = (x - hi.astype(F32)).astype(SPLIT_DTYPE)
    return hi, lo


def _dot3(ap, bp, dims):
    return _dot(ap[0], bp[0], dims) + (_dot(ap[0], bp[1], dims) + _dot(ap[1], bp[0], dims))


def _tri_sums(xs, tri):
    parts = []
    for x in xs:
        parts.extend(_split(x))
    out = _dot(jnp.concatenate(parts, axis=0), tri, ((1,), (0,)))
    n = xs[0].shape[0]
    return [out[2 * i * n:(2 * i + 1) * n] + out[(2 * i + 1) * n:(2 * i + 2) * n] for i in range(len(xs))]


def _split_wide(x):
    hi = x.astype(SPLIT_DTYPE).astype(F32)
    return hi, x - hi


def _lhs3(x, axis):
    hi, lo = _split_wide(x)
    return jnp.concatenate([hi, hi, lo], axis=axis).astype(SPLIT_DTYPE)


def _rhs3(x, axis):
    hi, lo = _split_wide(x)
    return jnp.concatenate([hi, lo, hi], axis=axis).astype(SPLIT_DTYPE)


def _tri_masks():
    row = lax.broadcasted_iota(jnp.int32, (QBLK, QBLK), 0)
    col = lax.broadcasted_iota(jnp.int32, (QBLK, QBLK), 1)
    return row, col


def _attn_fwd(u, n_seq, t_rows, cw, name):
    rows = u.shape[0]
    nb = t_rows // QBLK
    width = min(2 * cw, ATTN_FWD_LANES)
    n_hd = width // HEAD_DIM
    groups = (2 * cw) // width
    kcol, vcol = (2 * cw) // width, (4 * cw) // width

    def body(q_ref, k_ref, v_ref, o_ref):
        i = pl.program_id(2)
        row, col = _tri_masks()
        incl = (row >= col).astype(SPLIT_DTYPE)
        past = col < row
        sls = [slice(hd * HEAD_DIM, (hd + 1) * HEAD_DIM) for hd in range(n_hd)]
        q3s = [_lhs3(q_ref[:, sl] * (HEAD_DIM ** -0.5), 1) for sl in sls]

        def tile(j, carries, accs, masked):
            r0 = pl.multiple_of(j * QBLK, QBLK)
            zs = [_dot(q3s[hd], _rhs3(k_ref[pl.ds(r0, QBLK), sl], 1), ((1,), (1,))) for hd, sl in enumerate(sls)]
            lks = []
            for z in zs:
                lk = -(jnp.maximum(z, 0.0) + jnp.log1p(jnp.exp(-jnp.abs(z))))
                lks.append(jnp.where(past, lk, 0.0) if masked else lk)
            cls = _tri_sums(lks, incl)
            new_c, new_a = [], []
            for hd, sl in enumerate(sls):
                a = jnp.exp(zs[hd] + cls[hd] + carries[hd])
                if masked:
                    a = jnp.where(past, a, 0.0)
                v3 = _rhs3(v_ref[pl.ds(r0, QBLK), sl], 0)
                new_a.append(accs[hd] + _dot(_lhs3(a, 1), v3, ((1,), (0,))))
                new_c.append(carries[hd] + cls[hd][:, 0:1])
            return tuple(new_c), tuple(new_a)

        c0 = tuple(jnp.zeros((QBLK, 1), F32) for _ in sls)
        a0 = tuple(jnp.zeros((QBLK, HEAD_DIM), F32) for _ in sls)
        st = tile(i, c0, a0, True)
        st = lax.fori_loop(0, i, lambda jj, c: tile(i - 1 - jj, c[0], c[1], False), st)
        o_ref[...] = jnp.concatenate(st[1], axis=1)

    q_spec = pl.BlockSpec((QBLK, width), lambda s, p, i: (s * nb + i, p))
    k_spec = pl.BlockSpec((t_rows, width), lambda s, p, i: (s, kcol + p))
    v_spec = pl.BlockSpec((t_rows, width), lambda s, p, i: (s, vcol + p))
    return pl.pallas_call(
        body, name=name,
        out_shape=jax.ShapeDtypeStruct((rows, 2 * cw), F32),
        grid=(n_seq, groups, nb), in_specs=[q_spec, k_spec, v_spec],
        out_specs=pl.BlockSpec((QBLK, width), lambda s, p, i: (s * nb + i, p)),
        compiler_params=_cparams("parallel", "parallel", "arbitrary"),
    )(u, u, u)


def _attn_bwd(u, o, do, n_seq, t_rows, cw, name):
    rows = u.shape[0]
    nb = t_rows // QBLK
    width = min(2 * cw, ATTN_LANES)
    n_hd = width // HEAD_DIM
    groups = (2 * cw) // width
    kcol, vcol = (2 * cw) // width, (4 * cw) // width

    def body(q_ref, k_ref, v_ref, o_ref, do_ref, dq_ref, dk_ref, dv_ref, dk_acc, dv_acc):
        i = pl.program_id(2)
        row, col = _tri_masks()
        incl = (row >= col).astype(SPLIT_DTYPE)
        strict = (row > col).astype(SPLIT_DTYPE)
        past = col < row

        @pl.when(i == 0)
        def _():
            dk_acc[...] = jnp.zeros_like(dk_acc)
            dv_acc[...] = jnp.zeros_like(dv_acc)

        sls = [slice(hd * HEAD_DIM, (hd + 1) * HEAD_DIM) for hd in range(n_hd)]
        qs = [q_ref[:, sl] * (HEAD_DIM ** -0.5) for sl in sls]
        q3s = [_lhs3(q, 1) for q in qs]
        q3r = [_rhs3(q, 0) for q in qs]
        do3s = [_lhs3(do_ref[:, sl], 1) for sl in sls]
        do3r = [_rhs3(do_ref[:, sl], 0) for sl in sls]
        dtots = [jnp.sum(do_ref[:, sl] * o_ref[:, sl], axis=-1, keepdims=True) for sl in sls]

        def tile(j, ccs, cgs, dqs, masked):
            r0 = pl.multiple_of(j * QBLK, QBLK)
            ks = [k_ref[pl.ds(r0, QBLK), sl] for sl in sls]
            vs = [v_ref[pl.ds(r0, QBLK), sl] for sl in sls]
            zs = [_dot(q3s[hd], _rhs3(ks[hd], 1), ((1,), (1,))) for hd in range(n_hd)]
            das = [_dot(do3s[hd], _rhs3(vs[hd], 1), ((1,), (1,))) for hd in range(n_hd)]
            lks, sigs = [], []
            for z in zs:
                e = jnp.exp(-jnp.abs(z))
                lk = -(jnp.maximum(z, 0.0) + jnp.log1p(e))
                sigs.append(jnp.where(z >= 0.0, 1.0, e) / (1.0 + e))
                lks.append(jnp.where(past, lk, 0.0) if masked else lk)
            cls = _tri_sums(lks, incl)
            a_s, gs = [], []
            for hd in range(n_hd):
                a = jnp.exp(zs[hd] + cls[hd] + ccs[hd])
                if masked:
                    a = jnp.where(past, a, 0.0)
                a_s.append(a)
                gs.append(a * das[hd])
            gxs = _tri_sums(gs, strict)
            n_cc, n_cg, n_dq = [], [], []
            for hd in range(n_hd):
                dz = gs[hd] - sigs[hd] * (dtots[hd] - cgs[hd] - gxs[hd])
                if masked:
                    dz = jnp.where(past, dz, 0.0)
                n_dq.append(dqs[hd] + _dot(_lhs3(dz, 1), _rhs3(ks[hd], 0), ((1,), (0,))))
                dk_acc[hd, pl.ds(r0, QBLK), :] += _dot(_lhs3(dz, 0), q3r[hd], ((0,), (0,)))
                dv_acc[hd, pl.ds(r0, QBLK), :] += _dot(_lhs3(a_s[hd], 0), do3r[hd], ((0,), (0,)))
                n_cc.append(ccs[hd] + cls[hd][:, 0:1])
                n_cg.append(cgs[hd] + gxs[hd][:, 0:1] + gs[hd][:, 0:1])
            return tuple(n_cc), tuple(n_cg), tuple(n_dq)

        z1 = tuple(jnp.zeros((QBLK, 1), F32) for _ in sls)
        dq0 = tuple(jnp.zeros((QBLK, HEAD_DIM), F32) for _ in sls)
        st = tile(i, z1, z1, dq0, True)
        st = lax.fori_loop(0, i, lambda jj, c: tile(i - 1 - jj, c[0], c[1], c[2], False), st)
        dq_ref[...] = (jnp.concatenate(st[2], axis=1) * (HEAD_DIM ** -0.5)).astype(dq_ref.dtype)

        @pl.when(i == nb - 1)
        def _():
            dk_ref[...] = jnp.concatenate([dk_acc[h] for h in range(n_hd)], axis=1).astype(dk_ref.dtype)
            dv_ref[...] = jnp.concatenate([dv_acc[h] for h in range(n_hd)], axis=1).astype(dv_ref.dtype)

    blk = pl.BlockSpec((QBLK, width), lambda s, p, i: (s * nb + i, p))
    k_spec = pl.BlockSpec((t_rows, width), lambda s, p, i: (s, kcol + p))
    v_spec = pl.BlockSpec((t_rows, width), lambda s, p, i: (s, vcol + p))
    seq_blk = pl.BlockSpec((t_rows, width), lambda s, p, i: (s, p))
    shp = jax.ShapeDtypeStruct((rows, 2 * cw), MM_DTYPE)
    return pl.pallas_call(
        body, name=name, out_shape=(shp, shp, shp),
        grid=(n_seq, groups, nb), in_specs=[blk, k_spec, v_spec, blk, blk],
        out_specs=(blk, seq_blk, seq_blk),
        scratch_shapes=[pltpu.VMEM((n_hd, t_rows, HEAD_DIM), F32), pltpu.VMEM((n_hd, t_rows, HEAD_DIM), F32)],
        compiler_params=_cparams("parallel", "parallel", "arbitrary"),
    )(u, u, u, o, do)


def _shifted(xw, shift):
    n = xw.shape[0]
    s = shift % n
    return xw if s == 0 else pltpu.roll(xw, s, 0)


def _causal_conv(xw, w_ref, kw, shift_sign=1):
    acc = None
    for kk in range(kw):
        term = w_ref[kk:kk + 1, :] * _shifted(xw, shift_sign * (kw - 1 - kk))
        acc = term if acc is None else acc + term
    return acc


def _rms(o, g):
    r = lax.rsqrt(jnp.mean(o * o, axis=-1, keepdims=True) + RMS_EPS)
    return o * r * g, r


def _rms_bwd(o, g, dy):
    r = lax.rsqrt(jnp.mean(o * o, axis=-1, keepdims=True) + RMS_EPS)
    gdy = g * dy
    do = gdy * r - o * (r * r * r) * jnp.mean(gdy * o, axis=-1, keepdims=True)
    return do, dy * o * r


def _mix_specs(rows, cw, halo_prev, halo_next):
    tr = CONV_ROWS
    per = tr // HALO
    last = rows // HALO - 1

    def group(cb):
        specs = []
        if halo_prev:
            specs.append(pl.BlockSpec((HALO, cw), lambda i: (jnp.maximum(i * per - 1, 0), cb)))
        specs.append(pl.BlockSpec((tr, cw), lambda i: (i, cb)))
        if halo_next:
            specs.append(pl.BlockSpec((HALO, cw), lambda i: (jnp.minimum((i + 1) * per, last), cb)))
        return specs
    return group


def _window(prev_ref, cur_ref, first):
    return jnp.concatenate([jnp.where(first, 0.0, prev_ref[...]), cur_ref[...]], axis=0)


def _mix_fwd(u, o_sb, w31, b31, lng, lnb, w3, gmix, t_rows, cw, name):
    rows = u.shape[0]
    d = 4 * cw
    tr = CONV_ROWS
    tiles_per_seq = t_rows // tr
    k31, k3 = w31.shape[0], w3.shape[0]
    group = _mix_specs(rows, cw, True, False)

    def body(ap, ac, gp, gc, bc, cp, cc, hp, hc, osb_ref, w31_ref, b31_ref, lng_ref, lnb_ref, w3_ref, gm_ref, y_ref):
        first = (pl.program_id(0) % tiles_per_seq) == 0
        aw = _window(ap, ac, first)
        gw = _window(gp, gc, first)
        hcw = aw * jax.nn.sigmoid(gw)
        yc = (_causal_conv(hcw, w31_ref, k31) + b31_ref[...])[HALO:]
        mu = jnp.mean(yc, axis=-1, keepdims=True)
        xc = yc - mu
        var = jnp.mean(xc * xc, axis=-1, keepdims=True)
        lc = xc * lax.rsqrt(var + LN_EPS) * lng_ref[...] + lnb_ref[...]
        oc = lc * jax.nn.sigmoid(lc)
        pw = _window(cp, cc, first) * _window(hp, hc, first)
        osc = bc[...] * _causal_conv(pw, w3_ref, k3)[HALO:]
        gm = gm_ref[...]
        y_sb, _ = _rms(osb_ref[...], gm[:, :2 * cw])
        y_c, _ = _rms(oc, gm[:, 2 * cw:3 * cw])
        y_s, _ = _rms(osc, gm[:, 3 * cw:])
        y_ref[...] = jnp.concatenate([y_sb, y_c, y_s], axis=1).astype(y_ref.dtype)

    full = lambda arr: pl.BlockSpec(arr.shape, lambda i: (0, 0))
    small = [w31, b31.reshape(1, cw), lng.reshape(1, cw), lnb.reshape(1, cw), w3, gmix.reshape(1, d)]
    in_specs = (group(6) + group(7) + [pl.BlockSpec((tr, cw), lambda i: (i, 8))] + group(9) + group(10)
                + [pl.BlockSpec((tr, 2 * cw), lambda i: (i, 0))] + [full(s) for s in small])
    return pl.pallas_call(
        body, name=name, out_shape=jax.ShapeDtypeStruct((rows, d), MM_DTYPE),
        grid=(rows // tr,), in_specs=in_specs, out_specs=pl.BlockSpec((tr, d), lambda i: (i, 0)),
        compiler_params=_cparams("parallel"),
    )(u, u, u, u, u, u, u, u, u, o_sb, *small)


def _mix_bwd(u, o_sb, dy, w31, b31, lng, lnb, w3, gmix, t_rows, cw, name):
    rows = u.shape[0]
    d = 4 * cw
    tr = CONV_ROWS
    tiles_per_seq = t_rows // tr
    k31, k3 = w31.shape[0], w3.shape[0]
    group = _mix_specs(rows, cw, True, True)
    cur = slice(HALO, HALO + tr)

    def body(*refs):
        refs = list(refs)
        take = lambda n: [refs.pop(0) for _ in range(n)]
        a3, g3, b3, c3, h3, dyc3, dys3 = (take(3) for _ in range(7))
        osb_ref, dysb_ref, w31_ref, b31_ref, lng_ref, lnb_ref, w3_ref, gm_ref = take(8)
        dosb_ref, du_ref, dgm_ref, dw31_ref, db31_ref, dlng_ref, dlnb_ref, dw3_ref = refs
        i = pl.program_id(0)
        first = (i % tiles_per_seq) == 0
        last = (i % tiles_per_seq) == tiles_per_seq - 1
        wrow = lax.broadcasted_iota(jnp.int32, (tr + 2 * HALO, 1), 0)
        keep = jnp.logical_not((first & (wrow < HALO)) | (last & (wrow >= HALO + tr)))

        def win(r3):
            return jnp.concatenate([r3[0][...], r3[1][...], r3[2][...]], axis=0)

        gm = gm_ref[...]
        aw, gw, bw, cwin, hw = win(a3), win(g3), win(b3), win(c3), win(h3)
        dy_c = jnp.where(keep, win(dyc3), 0.0)
        dy_s = jnp.where(keep, win(dys3), 0.0)
        sg = jax.nn.sigmoid(gw)
        hcw = jnp.where(keep, aw * sg, 0.0)
        yc = _causal_conv(hcw, w31_ref, k31) + b31_ref[...]
        mu = jnp.mean(yc, axis=-1, keepdims=True)
        xc = yc - mu
        rstd = lax.rsqrt(jnp.mean(xc * xc, axis=-1, keepdims=True) + LN_EPS)
        nrm = xc * rstd
        lc = nrm * lng_ref[...] + lnb_ref[...]
        sl = jax.nn.sigmoid(lc)
        oc = lc * sl
        d_oc, dg_c = _rms_bwd(oc, gm[:, 2 * cw:3 * cw], dy_c)
        dlc = d_oc * sl * (1.0 + lc * (1.0 - sl))
        dn = dlc * lng_ref[...]
        dyc = rstd * (dn - jnp.mean(dn, axis=-1, keepdims=True) - nrm * jnp.mean(dn * nrm, axis=-1, keepdims=True))
        dyc = jnp.where(keep, dyc, 0.0)
        dhc = _causal_conv(dyc, w31_ref, k31, -1)
        da = dhc * sg
        dgate = dhc * aw * sg * (1.0 - sg)
        pw = jnp.where(keep, cwin * hw, 0.0)
        cs = _causal_conv(pw, w3_ref, k3)
        osc = bw * cs
        d_osc, dg_s = _rms_bwd(osc, gm[:, 3 * cw:], dy_s)
        dcs = jnp.where(keep, d_osc * bw, 0.0)
        dbg = d_osc * cs
        dp = _causal_conv(dcs, w3_ref, k3, -1)
        dcg = dp * hw
        dhs = dp * cwin
        du_ref[...] = jnp.concatenate([x[cur] for x in (da, dgate, dbg, dcg, dhs)], axis=1).astype(du_ref.dtype)
        d_osb, dg_sb = _rms_bwd(osb_ref[...], gm[:, :2 * cw], dysb_ref[...])
        dosb_ref[...] = d_osb

        @pl.when(i == 0)
        def _():
            for r in (dgm_ref, dw31_ref, db31_ref, dlng_ref, dlnb_ref, dw3_ref):
                r[...] = jnp.zeros_like(r)

        csum = lambda x: jnp.sum(x, axis=0, keepdims=True)
        dgm_ref[...] += jnp.concatenate([csum(dg_sb), csum(dg_c[cur]), csum(dg_s[cur])], axis=1)
        db31_ref[...] += csum(dyc[cur])
        dlng_ref[...] += csum((dlc * nrm)[cur])
        dlnb_ref[...] += csum(dlc[cur])
        for kk in range(k31):
            dw31_ref[kk:kk + 1, :] += csum((dyc * _shifted(hcw, k31 - 1 - kk))[cur])
        for kk in range(k3):
            dw3_ref[kk:kk + 1, :] += csum((dcs * _shifted(pw, k3 - 1 - kk))[cur])

    full = lambda arr: pl.BlockSpec(arr.shape, lambda i: (0, 0))
    small = [w31, b31.reshape(1, cw), lng.reshape(1, cw), lnb.reshape(1, cw), w3, gmix.reshape(1, d)]
    in_specs = (group(6) + group(7) + group(8) + group(9) + group(10) + group(2) + group(3)
                + [pl.BlockSpec((tr, 2 * cw), lambda i: (i, 0))] * 2 + [full(s) for s in small])
    acc = lambda r, c: (jax.ShapeDtypeStruct((r, c), F32), pl.BlockSpec((r, c), lambda i: (0, 0)))
    outs = [(jax.ShapeDtypeStruct((rows, 2 * cw), F32), pl.BlockSpec((tr, 2 * cw), lambda i: (i, 0))),
            (jax.ShapeDtypeStruct((rows, 5 * cw), MM_DTYPE), pl.BlockSpec((tr, 5 * cw), lambda i: (i, 0))),
            acc(1, d), acc(k31, cw), acc(1, cw), acc(1, cw), acc(1, cw), acc(k3, cw)]
    return pl.pallas_call(
        body, name=name, out_shape=tuple(o[0] for o in outs), grid=(rows // tr,), in_specs=in_specs,
        out_specs=tuple(o[1] for o in outs), compiler_params=_cparams("arbitrary"),
    )(*([u] * 15), *([dy] * 6), o_sb, dy, *small)


def _position():
    return lax.axis_index("x"), lax.axis_index("y"), lax.axis_index("c")


def _all_gather(xs, name):
    n_w = len(xs)
    per = N_DEV - 1

    def body(*refs):
        x_refs, out_refs = refs[:n_w], refs[n_w:2 * n_w]
        send_sems, recv_sems, local_sems = refs[2 * n_w:]
        mx, my, mc = _position()
        me, sibling = (mx, my, mc), (mx, my, 1 - mc)
        chips = [(1 - mx, my), (mx, 1 - my), (1 - mx, 1 - my)]
        ws = range(n_w)

        def slot(w, px, py, pc):
            return out_refs[w].at[4 * px + 2 * py + pc]

        def copy(w, k, block, to, own=False):
            return pltpu.make_async_remote_copy(
                src_ref=x_refs[w] if own else slot(w, *block), dst_ref=slot(w, *block),
                send_sem=send_sems.at[per * w + k], recv_sem=recv_sems.at[per * w + k],
                device_id=to, device_id_type=MESH_IDS)

        mine = [pltpu.make_async_copy(x_refs[w], slot(w, *me), local_sems.at[w]) for w in ws]
        for cp in mine:
            cp.start()
        first = [copy(w, 1 + j, me, (*chip, mc), own=True) for w in ws for j, chip in enumerate(chips)]
        first += [copy(w, 0, me, sibling, own=True) for w in ws]
        for cp in first:
            cp.start()
        passed = []
        for j, chip in enumerate(chips):
            for w in ws:
                copy(w, 1 + j, (*chip, mc), me).wait_recv()
                fwd = copy(w, 4 + j, (*chip, mc), sibling)
                fwd.start()
                passed.append(fwd)
        for w in ws:
            copy(w, 0, sibling, me).wait_recv()
        for j, chip in enumerate(chips):
            for w in ws:
                copy(w, 4 + j, (*chip, 1 - mc), me).wait_recv()
        for cp in first + passed:
            cp.wait_send()
        for cp in mine:
            cp.wait()

    hbm = pl.BlockSpec(memory_space=pl.ANY)
    return pl.pallas_call(
        body, name=name, out_shape=tuple(jax.ShapeDtypeStruct((N_DEV,) + x.shape, x.dtype) for x in xs),
        in_specs=[hbm] * n_w, out_specs=tuple([hbm] * n_w),
        scratch_shapes=[pltpu.SemaphoreType.DMA((per * n_w,)), pltpu.SemaphoreType.DMA((per * n_w,)),
                        pltpu.SemaphoreType.DMA((n_w,))],
    )(*xs)


def _all_to_all(sends, name):
    n_w = len(sends)

    def body(*refs):
        send_refs, recv_refs = refs[:n_w], refs[n_w:2 * n_w]
        send_sems, recv_sems = refs[2 * n_w:]
        mx, my, mc = _position()
        copies = []
        for k in range(1, N_DEV):
            px, py, pc = mx ^ (k >> 2), my ^ ((k >> 1) & 1), mc ^ (k & 1)
            for w in range(n_w):
                copies.append(pltpu.make_async_remote_copy(
                    src_ref=send_refs[w].at[4 * px + 2 * py + pc], dst_ref=recv_refs[w].at[k - 1],
                    send_sem=send_sems.at[w * (N_DEV - 1) + k - 1], recv_sem=recv_sems.at[w * (N_DEV - 1) + k - 1],
                    device_id=(px, py, pc), device_id_type=MESH_IDS))
        for cp in copies:
            cp.start()
        for cp in copies:
            cp.wait_recv()
        for cp in copies:
            cp.wait_send()

    hbm = pl.BlockSpec(memory_space=pl.ANY)
    n_sem = n_w * (N_DEV - 1)
    return pl.pallas_call(
        body, name=name,
        out_shape=tuple(jax.ShapeDtypeStruct((N_DEV - 1,) + s.shape[1:], s.dtype) for s in sends),
        in_specs=[hbm] * n_w, out_specs=tuple([hbm] * n_w),
        scratch_shapes=[pltpu.SemaphoreType.DMA((n_sem,)), pltpu.SemaphoreType.DMA((n_sem,))],
    )(*sends)


def _adam_math(w, g, m, v):
    m = ADAM_B1 * m + (1.0 - ADAM_B1) * g
    v = ADAM_B2 * v + (1.0 - ADAM_B2) * (g * g)
    m_hat = m / (1.0 - ADAM_B1 ** ADAM_STEP)
    v_hat = v / (1.0 - ADAM_B2 ** ADAM_STEP)
    delta = -ADAM_LR * (m_hat / (jnp.sqrt(v_hat) + ADAM_EPS) + ADAM_WD * w)
    return delta, m, v


def _adam(g_own, recv, w, m, v, name):
    rows, cols = w.shape
    tr = _tile(rows, max(8, ADAM_BLOCK_ELEMS // cols), 8)

    def body(*refs):
        if recv is not None:
            g_ref, r_ref, w_ref, m_ref, v_ref, go_ref, d_ref, mo_ref, vo_ref = refs
            g = g_ref[...]
            for k in range(recv.shape[0]):
                g = g + r_ref[k].astype(F32)
        else:
            g_ref, w_ref, m_ref, v_ref, go_ref, d_ref, mo_ref, vo_ref = refs
            g = g_ref[...]
        delta, mn, vn = _adam_math(w_ref[...], g, m_ref[...], v_ref[...])
        go_ref[...] = g
        d_ref[...] = delta
        mo_ref[...] = mn
        vo_ref[...] = vn

    blk = pl.BlockSpec((tr, cols), lambda i: (i, 0))
    in_specs = [blk] + ([pl.BlockSpec((recv.shape[0], tr, cols), lambda i: (0, i, 0))] if recv is not None else []) + [blk] * 3
    args = [g_own] + ([recv] if recv is not None else []) + [w, m, v]
    shp = jax.ShapeDtypeStruct((rows, cols), F32)
    return pl.pallas_call(
        body, name=name, out_shape=(shp,) * 4, grid=(rows // tr,), in_specs=in_specs, out_specs=(blk,) * 4,
        compiler_params=_cparams("parallel"),
    )(*args)


def _sum_devices(parts, name):
    n, rows, cols = parts.shape

    def body(p_ref, o_ref):
        acc = p_ref[0]
        for k in range(1, n):
            acc = acc + p_ref[k]
        o_ref[...] = acc

    return pl.pallas_call(
        body, name=name, out_shape=jax.ShapeDtypeStruct((rows, cols), F32),
        in_specs=[pl.BlockSpec(memory_space=pltpu.VMEM)], out_specs=pl.BlockSpec(memory_space=pltpu.VMEM),
    )(parts)


def _pack(arrs, dtype):
    flat = jnp.concatenate([a.reshape(-1).astype(dtype) for a in arrs])
    pad = (-flat.shape[0]) % (16 * LANE)
    return jnp.pad(flat, (0, pad)).reshape(-1, LANE)


def _unpack(flat2d, shapes):
    flat = flat2d.reshape(-1)
    out, off = [], 0
    for s in shapes:
        n = 1
        for dim in s:
            n *= dim
        out.append(flat[off:off + n].reshape(s))
        off += n
    return out


def _unpack_gathered(g, shapes):
    flat = g.reshape(N_DEV, -1)
    out, off = [], 0
    for s in shapes:
        n = 1
        for dim in s:
            n *= dim
        out.append(flat[:, off:off + n].reshape((N_DEV,) + tuple(s)))
        off += n
    return out


def _cols_from_devices(g):
    nd = g.ndim
    perm = tuple(range(1, nd - 1)) + (0, nd - 1)
    t = jnp.transpose(g, perm)
    return t.reshape(t.shape[:-2] + (t.shape[-2] * t.shape[-1],))


def _cols_to_devices(a, dtype):
    c = a.shape[-1] // N_DEV
    t = a.reshape(a.shape[:-1] + (N_DEV, c)).astype(dtype)
    nd = t.ndim
    return jnp.transpose(t, (nd - 2,) + tuple(range(nd - 2)) + (nd - 1,))


def _rows_to_devices(a, dtype):
    nl, r8, c = a.shape
    return jnp.transpose(a.reshape(nl, N_DEV, r8 // N_DEV, c).astype(dtype), (1, 0, 2, 3))


def kernel(x, meta_tokens, ln_in_g, ln_in_b, w_in, w_conf_dw, b_conf_dw, ln_conf_g, ln_conf_b, w_short_dw, g_mix, w_out, ln_mix_g, ln_mix_b, w_ff1, w_ff2, ln_ff_g, ln_ff_b, loss_target, m_meta_tokens, m_ln_in_g, m_ln_in_b, m_w_in, m_w_conf_dw, m_b_conf_dw, m_ln_conf_g, m_ln_conf_b, m_w_short_dw, m_g_mix, m_w_out, m_ln_mix_g, m_ln_mix_b, m_w_ff1, m_w_ff2, m_ln_ff_g, m_ln_ff_b, v_meta_tokens, v_ln_in_g, v_ln_in_b, v_w_in, v_w_conf_dw, v_b_conf_dw, v_ln_conf_g, v_ln_conf_b, v_w_short_dw, v_g_mix, v_w_out, v_ln_mix_g, v_ln_mix_b, v_w_ff1, v_w_ff2, v_ln_ff_g, v_ln_ff_b):
    n_seq, seq, d = x.shape
    depth = w_in.shape[0]
    cw = d // 4
    alpha = (2.0 * depth) ** 0.25
    t_rows = -(-(N_META + seq) // QBLK) * QBLK
    rows = n_seq * t_rows
    tr = _row_tile(t_rows)
    tiles_per_seq = t_rows // tr
    me = 4 * lax.axis_index("x") + 2 * lax.axis_index("y") + lax.axis_index("c")

    small_shards = [meta_tokens, w_conf_dw, w_short_dw]
    sm = _unpack_gathered(_all_gather([_pack(small_shards, F32)], "gather_small_params")[0],
                          [a.shape for a in small_shards])
    meta_full, w31_full, w3_full = (_cols_from_devices(a) for a in sm)
    big_shards = [w_in, w_out, w_ff1, w_ff2]
    bg = _all_gather(big_shards, "gather_weights")
    win_f = _cols_from_devices(bg[0])
    wout_f = jnp.transpose(bg[1], (1, 0, 2, 3)).reshape(depth, d, d)
    wff1_f = _cols_from_devices(bg[2])
    wff2_f = jnp.transpose(bg[3], (1, 0, 2, 3)).reshape(depth, 4 * d, d)

    pad_rows = t_rows - N_META - seq
    xin = jnp.concatenate([jnp.broadcast_to(meta_full[None], (n_seq, N_META, d)), x,
                           jnp.zeros((n_seq, pad_rows, d), F32)], axis=1).reshape(rows, d)
    tgt = jnp.pad(loss_target, ((0, 0), (N_META, pad_rows), (0, 0))).reshape(rows, d)
    h, hb = _ln_fwd(xin, None, 1.0, ln_in_g, ln_in_b, tr, "ln_in")
    saved = []
    for l in range(depth):
        u = _matmul(hb, win_f, "nn", F32, "proj_in", layer=l, pn=1408)
        o_sb = _attn_fwd(u, n_seq, t_rows, cw, "attn_fwd")
        yb = _mix_fwd(u, o_sb, w31_full[l], b_conf_dw[l], ln_conf_g[l], ln_conf_b[l], w3_full[l], g_mix[l],
                      t_rows, cw, "mix_fwd")
        mix = _matmul(yb, wout_f, "nn", F32, "proj_out", layer=l)
        h1, h1b = _ln_fwd(h, mix, alpha, ln_mix_g[l], ln_mix_b[l], tr, "ln_mix")
        act = _matmul(h1b, wff1_f, "nn", MM_DTYPE, "ff1", layer=l, epilogue="relu2")
        ff = _matmul(act, wff2_f, "nn", F32, "ff2", layer=l)
        h2, h2b = _ln_fwd(h1, ff, alpha, ln_ff_g[l], ln_ff_b[l], tr, "ln_ff")
        saved.append((h, hb, u, o_sb, yb, mix, h1, h1b, act, ff))
        h, hb = h2, h2b
    loss_part, dy1 = _loss(h, tgt, seq, tr, tiles_per_seq, "loss")
    loss = lax.psum(loss_part[0, 0], MESH_AXES)

    dy2, a_dy = None, 1.0
    gw_in, gw_out, gw_ff1, gw_ff2 = [], [], [], []
    g_small = {n: [] for n in ("w31", "b31", "lng", "lnb", "w3", "gmix", "lmg", "lmb", "lfg", "lfb")}
    for l in reversed(range(depth)):
        h, hb, u, o_sb, yb, mix, h1, h1b, act, ff = saved[l]
        dr2, dr2b, dg, db = _ln_bwd(h1, ff, alpha, ln_ff_g[l], dy1, dy2, a_dy, tr, "ln_ff_bwd")
        g_small["lfg"].append(dg)
        g_small["lfb"].append(db)
        df1b = _matmul(dr2b, wff2_f, "nt", MM_DTYPE, "ff2_dx", layer=l, epilogue="dsqrelu", extra=act)
        gw_ff2.append(_matmul(act, dr2b, "tn", F32, "ff2_dw", pm=1024, pk=544))
        dh1 = _matmul(df1b, wff1_f, "nt", F32, "ff1_dx", layer=l)
        gw_ff1.append(_matmul(h1b, df1b, "tn", F32, "ff1_dw", pm=1024, pk=544))
        dr1, dr1b, dg, db = _ln_bwd(h, mix, alpha, ln_mix_g[l], dr2, dh1, alpha, tr, "ln_mix_bwd")
        g_small["lmg"].append(dg)
        g_small["lmb"].append(db)
        dyr = _matmul(dr1b, wout_f, "nt", F32, "proj_out_dx", layer=l)
        gw_out.append(_matmul(yb, dr1b, "tn", F32, "proj_out_dw", pm=1024, pk=544))
        d_osb, du_conv, dgm, dw31, db31, dlng, dlnb, dw3 = _mix_bwd(
            u, o_sb, dyr, w31_full[l], b_conf_dw[l], ln_conf_g[l], ln_conf_b[l], w3_full[l], g_mix[l],
            t_rows, cw, "mix_bwd")
        for n, val in zip(("gmix", "w31", "b31", "lng", "lnb", "w3"), (dgm, dw31, db31, dlng, dlnb, dw3)):
            g_small[n].append(val)
        dq, dk, dv = _attn_bwd(u, o_sb, d_osb, n_seq, t_rows, cw, "attn_bwd")
        du = jnp.concatenate([dq, dk, dv, du_conv], axis=1)
        dh = _matmul(du, win_f, "nt", F32, "proj_in_dx", layer=l, pk=1408)
        gw_in.append(_matmul(hb, du, "tn", F32, "proj_in_dw", pm=1024, pn=1408, pk=544))
        dy1, dy2, a_dy = dr1, dh, alpha
    dxin, _, dg_in, db_in, dmeta = _ln_bwd(xin, None, 1.0, ln_in_g, dy1, dy2, a_dy, tr, "ln_in_bwd",
                                           meta_rows=N_META, tiles_per_seq=tiles_per_seq)
    grad_x = dxin.reshape(n_seq, t_rows, d)[:, N_META:N_META + seq]

    stack = lambda lst: jnp.stack(lst[::-1])
    gw_in, gw_out, gw_ff1, gw_ff2 = stack(gw_in), stack(gw_out), stack(gw_ff1), stack(gw_ff2)
    sends = [_cols_to_devices(gw_in, COMM_GRAD_DTYPE), _rows_to_devices(gw_out, COMM_GRAD_DTYPE),
             _cols_to_devices(gw_ff1, COMM_GRAD_DTYPE), _rows_to_devices(gw_ff2, COMM_GRAD_DTYPE)]
    recvs = _all_to_all(sends, "exchange_weight_grads")
    own = [lax.dynamic_slice_in_dim(gw_in, me * w_in.shape[2], w_in.shape[2], 2),
           lax.dynamic_slice_in_dim(gw_out, me * w_out.shape[1], w_out.shape[1], 1),
           lax.dynamic_slice_in_dim(gw_ff1, me * w_ff1.shape[2], w_ff1.shape[2], 2),
           lax.dynamic_slice_in_dim(gw_ff2, me * w_ff2.shape[1], w_ff2.shape[1], 1)]
    big = {}
    for nm, g_own, rv, w, m, v in zip(("w_in", "w_out", "w_ff1", "w_ff2"), own, recvs,
                                      (w_in, w_out, w_ff1, w_ff2), (m_w_in, m_w_out, m_w_ff1, m_w_ff2),
                                      (v_w_in, v_w_out, v_w_ff1, v_w_ff2)):
        c = w.shape[-1]
        res = _adam(g_own.reshape(-1, c), rv.reshape(N_DEV - 1, -1, c), w.reshape(-1, c), m.reshape(-1, c),
                    v.reshape(-1, c), "adam_" + nm)
        big[nm] = [r.reshape(w.shape) for r in res]

    sstack = lambda n: jnp.stack(g_small[n][::-1])
    partial = {
        "meta_tokens": dmeta, "ln_in_g": dg_in.reshape(d), "ln_in_b": db_in.reshape(d),
        "w_conf_dw": sstack("w31"), "b_conf_dw": sstack("b31").reshape(depth, cw),
        "ln_conf_g": sstack("lng").reshape(depth, cw), "ln_conf_b": sstack("lnb").reshape(depth, cw),
        "w_short_dw": sstack("w3"), "g_mix": sstack("gmix").reshape(depth, d),
        "ln_mix_g": sstack("lmg").reshape(depth, d), "ln_mix_b": sstack("lmb").reshape(depth, d),
        "ln_ff_g": sstack("lfg").reshape(depth, d), "ln_ff_b": sstack("lfb").reshape(depth, d),
    }
    names = list(partial)
    gathered = _all_gather([_pack([partial[n] for n in names], F32)], "gather_small_grads")[0]
    summed = _unpack(_sum_devices(gathered, "sum_small_grads"), [partial[n].shape for n in names])
    g_full = dict(zip(names, summed))
    local = {"meta_tokens": (meta_tokens, m_meta_tokens, v_meta_tokens), "ln_in_g": (ln_in_g, m_ln_in_g, v_ln_in_g),
             "ln_in_b": (ln_in_b, m_ln_in_b, v_ln_in_b), "w_conf_dw": (w_conf_dw, m_w_conf_dw, v_w_conf_dw),
             "b_conf_dw": (b_conf_dw, m_b_conf_dw, v_b_conf_dw), "ln_conf_g": (ln_conf_g, m_ln_conf_g, v_ln_conf_g),
             "ln_conf_b": (ln_conf_b, m_ln_conf_b, v_ln_conf_b), "w_short_dw": (w_short_dw, m_w_short_dw, v_w_short_dw),
             "g_mix": (g_mix, m_g_mix, v_g_mix), "ln_mix_g": (ln_mix_g, m_ln_mix_g, v_ln_mix_g),
             "ln_mix_b": (ln_mix_b, m_ln_mix_b, v_ln_mix_b), "ln_ff_g": (ln_ff_g, m_ln_ff_g, v_ln_ff_g),
             "ln_ff_b": (ln_ff_b, m_ln_ff_b, v_ln_ff_b)}
    g_loc = {}
    for n in names:
        wloc = local[n][0]
        g = g_full[n]
        if g.shape != wloc.shape:
            g = lax.dynamic_slice_in_dim(g, me * wloc.shape[-1], wloc.shape[-1], g.ndim - 1)
        g_loc[n] = g
    shapes = [g_loc[n].shape for n in names]
    _, sd, smo, svo = _adam(_pack([g_loc[n] for n in names], F32), None, _pack([local[n][0] for n in names], F32),
                            _pack([local[n][1] for n in names], F32), _pack([local[n][2] for n in names], F32),
                            "adam_small")
    small_d, small_m, small_v = (dict(zip(names, _unpack(a, shapes))) for a in (sd, smo, svo))

    order = ["meta_tokens", "ln_in_g", "ln_in_b", "w_in", "w_conf_dw", "b_conf_dw", "ln_conf_g", "ln_conf_b",
             "w_short_dw", "g_mix", "w_out", "ln_mix_g", "ln_mix_b", "w_ff1", "w_ff2", "ln_ff_g", "ln_ff_b"]
    grads = [big[n][0] if n in big else g_loc[n] for n in order]
    deltas = [big[n][1] if n in big else small_d[n] for n in order]
    new_m = [big[n][2] if n in big else small_m[n] for n in order]
    new_v = [big[n][3] if n in big else small_v[n] for n in order]
    return (loss, grad_x, *grads, *deltas, *new_m, *new_v)
```
